```python
import math
import jax
import jax.numpy as jnp
from jax import lax
import numpy as np

D_MODEL = 1024
BATCH = 8
SEQ = 2048
DEPTH = 2
DEC_BATCH = 128
DEC_SEQ = 8
PAST_LEN = 2048
PAGE_SIZE = 128

HEAD_DIM = 64
SB_WIDTH = D_MODEL // 2
SB_HEADS = SB_WIDTH // HEAD_DIM
SB_BLOCK = 128
SB_COLS = 3 * SB_WIDTH
SB_BIAS_INIT = -8.0
RW_WIDTH = D_MODEL // 4
RW_HEADS = RW_WIDTH // HEAD_DIM
RW_DECAY_LORA = 64
RW_AAA_LORA = 64
RW_GATE_LORA = 160
RW_GN_EPS = HEAD_DIM * 1e-5
RW_COLS = 3 * RW_WIDTH + RW_DECAY_LORA + RW_AAA_LORA + RW_GATE_LORA
RW_SPLITS = [RW_WIDTH, 2 * RW_WIDTH, 3 * RW_WIDTH, 3 * RW_WIDTH + RW_DECAY_LORA,
             3 * RW_WIDTH + RW_DECAY_LORA + RW_AAA_LORA]
HG_WIDTH = D_MODEL // 4
HG_HEADS = HG_WIDTH // HEAD_DIM
HG_CHUNK = 64
HG_COLS = 4 * HG_WIDTH
N_BRANCH = 3
GATE_COLS = N_BRANCH * D_MODEL
N_IN = SB_COLS + RW_COLS + HG_COLS + GATE_COLS
N_EXPERTS = 32
TOP_K = 4
D_EXPERT = D_MODEL
SWIGLU_LIMIT = 7.0
SWIGLU_ALPHA = 1.702
NORM_EPS = 1e-5

kernel_name = 'hybrid_sb_rwkv7_hgrn2_moe_step'


def _rmsnorm(x, g):
    x32 = x.astype(jnp.float32)
    y = x32 * lax.rsqrt(jnp.mean(x32 * x32, axis=-1, keepdims=True) + NORM_EPS)
    return (y * g.astype(jnp.float32)).astype(x.dtype)


def _heads(a, n_heads):
    return a.reshape(a.shape[:-1] + (n_heads, a.shape[-1] // n_heads))


def _sb_attention(q, k, v, bias, q_pos0):
    lq = q.shape[1]
    blk = SB_BLOCK if lq % SB_BLOCK == 0 else lq
    q32 = q.astype(jnp.float32) * (HEAD_DIM ** -0.5)
    k32 = k.astype(jnp.float32)
    v32 = v.astype(jnp.float32)
    b32 = bias.astype(jnp.float32)[None, :, None, None]
    outs = []
    for start in range(0, lq, blk):
        n_keys = q_pos0 + start + blk
        z = jnp.einsum('bqhd,bkhd->bhqk', q32[:, start:start + blk], k32[:, :n_keys]) + b32
        q_pos = q_pos0 + start + jnp.arange(blk)
        readable = jnp.arange(n_keys)[None, :] < q_pos[:, None]
        log_beta = jax.nn.log_sigmoid(z)
        log_skip = jnp.where(readable, log_beta - z, 0.0)
        log_w = log_beta + lax.cumsum(log_skip, axis=3, reverse=True) - log_skip
        w = jnp.where(readable, jnp.exp(log_w), 0.0)
        outs.append(jnp.einsum('bhqk,bkhd->bqhd', w, v32[:, :n_keys]))
    return jnp.concatenate(outs, axis=1).astype(q.dtype)


def _rwkv7(p_rw, shift_row, s0, mu, w0, w_up, a0, a_up, g_up, k_k, k_a, r_k, ln_w, ln_b):
    f32 = jnp.float32
    B, T, _ = p_rw.shape
    p = p_rw.astype(f32)
    prev = jnp.concatenate([shift_row.astype(f32)[:, None, :], p[:, :-1]], axis=1)
    xs = p + (prev - p) * mu.astype(f32)
    r, k, v, wd, ad, gd = jnp.split(xs, RW_SPLITS, axis=-1)
    w_log = -jax.nn.softplus(-(w0 + jnp.tanh(wd) @ w_up)) - 0.5
    decay = jnp.exp(-jnp.exp(w_log))
    a = jax.nn.sigmoid(a0 + ad @ a_up)
    g = jax.nn.sigmoid(gd) @ g_up
    kk = _heads(k * k_k, RW_HEADS)
    kk = kk / jnp.maximum(jnp.sqrt(jnp.sum(kk * kk, axis=-1, keepdims=True)), 1e-12)
    k = k * (1.0 + (a - 1.0) * k_a)
    r, k, v, a, decay = [_heads(t, RW_HEADS) for t in (r, k, v, a, decay)]

    def step(S, inp):
        r_t, w_t, k_t, v_t, kk_t, a_t = inp
        sa = jnp.einsum('bhvk,bhk->bhv', S, -kk_t)
        S = (S * w_t[:, :, None, :] + sa[..., None] * (kk_t * a_t)[:, :, None, :]
             + v_t[..., None] * k_t[:, :, None, :])
        return S, jnp.einsum('bhvk,bhk->bhv', S, r_t)

    seq = tuple(jnp.moveaxis(t, 1, 0) for t in (r, decay, k, v, kk, a))
    S, y = lax.scan(step, s0.astype(f32), seq)
    y = jnp.moveaxis(y, 0, 1)
    mean = jnp.mean(y, axis=-1, keepdims=True)
    var = jnp.mean(jnp.square(y - mean), axis=-1, keepdims=True)
    y = ((y - mean) * lax.rsqrt(var + RW_GN_EPS)).reshape(B, T, RW_WIDTH) * ln_w + ln_b
    bonus = jnp.sum(r * k * r_k, axis=-1, keepdims=True) * v
    out = (y + bonus.reshape(B, T, RW_WIDTH)) * g
    return out.astype(p_rw.dtype), S.astype(p_rw.dtype), p_rw[:, -1]


def _gla_chunked(q, k, v, log_f, s0, chunk):
    B, T, H, _ = q.shape
    n = T // chunk

    def chunks(t):
        return jnp.moveaxis(t.reshape(B, n, chunk, H, t.shape[-1]), 1, 0)

    causal = jnp.tril(jnp.ones((chunk, chunk), dtype=bool))[None, :, :, None, None]

    def step(S, inp):
        qc, kc, vc, lf = inp
        b = jnp.cumsum(lf, axis=1)
        inter = jnp.einsum('bthk,bhkv->bthv', qc * jnp.exp(b), S)
        decay = jnp.exp(jnp.where(causal, b[:, :, None] - b[:, None, :], -jnp.inf))
        att = jnp.einsum('bthk,btshk,bshk->bhts', qc, decay, kc)
        intra = jnp.einsum('bhts,bshv->bthv', att, vc)
        b_end = b[:, -1]
        S = (jnp.exp(b_end)[..., None] * S
             + jnp.einsum('bshk,bshv->bhkv', kc * jnp.exp(b_end[:, None] - b), vc))
        return S, inter + intra

    S, o = lax.scan(step, s0, (chunks(q), chunks(k), chunks(v), chunks(log_f)))
    return jnp.moveaxis(o, 0, 1).reshape(B, T, H, v.shape[-1]), S


def _hgrn2(p_hg, s0, lb, norm_g):
    f32 = jnp.float32
    B, T, _ = p_hg.shape
    f_pre, i_in, q_in, g_out = jnp.split(p_hg.astype(f32), 4, axis=-1)
    log_f = jnp.logaddexp(jnp.log(lb), jnp.log1p(-lb) + jax.nn.log_sigmoid(f_pre))
    key = (1.0 - lb) * jax.nn.sigmoid(-f_pre)
    chunk = math.gcd(T, HG_CHUNK)
    o, S = _gla_chunked(_heads(q_in, HG_HEADS), _heads(key, HG_HEADS), _heads(i_in, HG_HEADS),
                        _heads(log_f, HG_HEADS), s0.astype(f32), chunk)
    o = o * lax.rsqrt(jnp.mean(o * o, axis=-1, keepdims=True) + NORM_EPS) * norm_g.astype(f32)
    o = o.reshape(B, T, HG_WIDTH) * jax.nn.sigmoid(g_out)
    return o.astype(p_hg.dtype), S.astype(p_hg.dtype)


def _moe(h, router_w, router_b, w1, b1, w2, b2):
    shp = h.shape
    ht = h.reshape(-1, shp[-1])
    logits = (ht @ router_w + router_b).astype(jnp.float32)
    top_val, top_idx = lax.top_k(logits, TOP_K)
    top_p = jax.nn.softmax(top_val, axis=-1)
    gates = jnp.sum(jax.nn.one_hot(top_idx, N_EXPERTS, dtype=jnp.float32) * top_p[..., None], axis=1)
    out = jnp.zeros(ht.shape, jnp.float32)
    for e in range(N_EXPERTS):
        gu = ht @ w1[e] + b1[e]
        glu = jnp.minimum(gu[:, :D_EXPERT], SWIGLU_LIMIT)
        lin = jnp.clip(gu[:, D_EXPERT:], -SWIGLU_LIMIT, SWIGLU_LIMIT)
        act = glu * jax.nn.sigmoid(SWIGLU_ALPHA * glu) * (lin + 1.0)
        out = out + gates[:, e:e + 1] * (act @ w2[e] + b2[e])
    return out.astype(h.dtype).reshape(shp)


def _run_group(x, c, read_past, rw_state, rw_shift, hg_state, pos0, prm):
    B, T, _ = x.shape
    lb_all = jnp.cumsum(jax.nn.softmax(prm['hg_lb_logits'].astype(jnp.float32), axis=0), axis=0)
    lb_all = lb_all - lb_all[0:1]
    c_act = jax.nn.silu(c)
    split_at = [SB_COLS, SB_COLS + RW_COLS, SB_COLS + RW_COLS + HG_COLS]
    ks, vs, rws, shs, hgs = [], [], [], [], []
    for l in range(DEPTH):
        mod = (c_act @ prm['w_ada'][l] + prm['b_ada'][l])[:, None, :]
        sh_a, sc_a, gt_a, sh_f, sc_f, gt_f = jnp.split(mod, 6, axis=-1)
        h = _rmsnorm(x, prm['norm_attn_g'][l]) * (1.0 + sc_a) + sh_a
        proj = h @ prm['w_in'][l]
        p_sb, p_rw, p_hg, p_gate = jnp.split(proj, split_at, axis=-1)
        q, k, v = [_heads(t, SB_HEADS) for t in jnp.split(p_sb, 3, axis=-1)]
        past_k, past_v = read_past(l)
        k_all = jnp.concatenate([past_k.astype(k.dtype), k], axis=1)
        v_all = jnp.concatenate([past_v.astype(v.dtype), v], axis=1)
        o_sb = _sb_attention(q, k_all, v_all, prm['sb_bias'][l], pos0).reshape(B, T, SB_WIDTH)
        o_rw, s_rw, sh_rw = _rwkv7(p_rw, rw_shift[l], rw_state[l], prm['rw_mu'][l], prm['rw_w0'][l],
                                   prm['rw_w_up'][l], prm['rw_a0'][l], prm['rw_a_up'][l], prm['rw_g_up'][l],
                                   prm['rw_k_k'][l], prm['rw_k_a'][l], prm['rw_r_k'][l],
                                   prm['rw_ln_w'][l], prm['rw_ln_b'][l])
        o_hg, s_hg = _hgrn2(p_hg, hg_state[l], lb_all[l], prm['hg_norm_g'][l])
        g_sb, g_rw, g_hg = jnp.split(jax.nn.sigmoid(p_gate), 3, axis=-1)
        merged = (g_sb * (o_sb @ prm['w_branch_sb'][l]) + g_rw * (o_rw @ prm['w_branch_rw'][l])
                  + g_hg * (o_hg @ prm['w_branch_hg'][l]))
        x = x + gt_a * (merged @ prm['w_out'][l])
        h2 = _rmsnorm(x, prm['norm_ffn_g'][l]) * (1.0 + sc_f) + sh_f
        x = x + gt_f * _moe(h2, prm['router_w'][l], prm['router_b'][l], prm['moe_w1'][l],
                            prm['moe_b1'][l], prm['moe_w2'][l], prm['moe_b2'][l])
        ks.append(k)
        vs.append(v)
        rws.append(s_rw)
        shs.append(sh_rw)
        hgs.append(s_hg)
    y = _rmsnorm(x, prm['final_norm_g'])
    return y, jnp.stack(ks), jnp.stack(vs), jnp.stack(rws), jnp.stack(shs), jnp.stack(hgs)


def setup_inputs(seed: int = 0) -> dict:
    key = jax.random.key(seed)
    keys = iter(jax.random.split(key, 48))

    def nrm(shape, scale):
        return jax.random.normal(next(keys), shape, jnp.float32) * scale

    def uni(shape, lo, hi):
        return jax.random.uniform(next(keys), shape, jnp.float32, lo, hi)

    D = D_MODEL
    n_pages = PAST_LEN // PAGE_SIZE
    n_used = DEC_BATCH * n_pages
    n_pool = n_used + max(1, n_used // 4)
    x_prompt = nrm((BATCH, SEQ, D), 1.0)
    x_sample = nrm((DEC_BATCH, DEC_SEQ, D), 1.0)
    cache_sb_k = nrm((DEPTH, n_pool, PAGE_SIZE, SB_HEADS, HEAD_DIM), 1.0)
    cache_sb_v = nrm((DEPTH, n_pool, PAGE_SIZE, SB_HEADS, HEAD_DIM), 1.0)
    state_rwkv = nrm((DEPTH, DEC_BATCH, RW_HEADS, HEAD_DIM, HEAD_DIM), 0.5)
    state_rwkv_shift = nrm((DEPTH, DEC_BATCH, RW_COLS), 1.0)
    state_hgrn = nrm((DEPTH, DEC_BATCH, HG_HEADS, HEAD_DIM, HEAD_DIM), 0.5)
    page_table = jax.random.permutation(next(keys), n_pool)[:n_used].reshape(DEC_BATCH, n_pages).astype(jnp.int32)
    return {
        'x_prompt': x_prompt,
        'x_sample': x_sample,
        'cache_sb_k': cache_sb_k,
        'cache_sb_v': cache_sb_v,
        'state_rwkv': state_rwkv,
        'state_rwkv_shift': state_rwkv_shift,
        'state_hgrn': state_hgrn,
        'page_table': page_table,
        'c_prompt': nrm((BATCH, D), 1.0),
        'c_sample': nrm((DEC_BATCH, D), 1.0),
        'w_ada': nrm((DEPTH, D, 6 * D), 0.5 * D ** -0.5),
        'b_ada': nrm((DEPTH, 6 * D), 0.02),
        'norm_attn_g': 1.0 + nrm((DEPTH, D), 0.05),
        'norm_ffn_g': 1.0 + nrm((DEPTH, D), 0.05),
        'w_in': nrm((DEPTH, D, N_IN), D ** -0.5),
        'sb_bias': SB_BIAS_INIT + nrm((DEPTH, SB_HEADS), 0.5),
        'rw_mu': uni((DEPTH, RW_COLS), 0.0, 1.0),
        'rw_w0': uni((DEPTH, RW_WIDTH), -4.0, -0.5),
        'rw_w_up': nrm((DEPTH, RW_DECAY_LORA, RW_WIDTH), 0.1 * RW_DECAY_LORA ** -0.5),
        'rw_a0': nrm((DEPTH, RW_WIDTH), 0.5),
        'rw_a_up': nrm((DEPTH, RW_AAA_LORA, RW_WIDTH), 0.5 * RW_AAA_LORA ** -0.5),
        'rw_g_up': nrm((DEPTH, RW_GATE_LORA, RW_WIDTH), RW_GATE_LORA ** -0.5),
        'rw_k_k': 0.85 + nrm((DEPTH, RW_WIDTH), 0.1),
        'rw_k_a': 1.0 + nrm((DEPTH, RW_WIDTH), 0.1),
        'rw_r_k': nrm((DEPTH, RW_HEADS, HEAD_DIM), 0.1),
        'rw_ln_w': 1.0 + nrm((DEPTH, RW_WIDTH), 0.05),
        'rw_ln_b': nrm((DEPTH, RW_WIDTH), 0.02),
        'hg_lb_logits': nrm((DEPTH, HG_WIDTH), 0.5),
        'hg_norm_g': 1.0 + nrm((DEPTH, HEAD_DIM), 0.05),
        'w_branch_sb': nrm((DEPTH, SB_WIDTH, D), SB_WIDTH ** -0.5),
        'w_branch_rw': nrm((DEPTH, RW_WIDTH, D), RW_WIDTH ** -0.5),
        'w_branch_hg': nrm((DEPTH, HG_WIDTH, D), HG_WIDTH ** -0.5),
        'w_out': nrm((DEPTH, D, D), D ** -0.5),
        'router_w': nrm((DEPTH, D, N_EXPERTS), D ** -0.5),
        'router_b': nrm((DEPTH, N_EXPERTS), 0.01),
        'moe_w1': nrm((DEPTH, N_EXPERTS, D, 2 * D_EXPERT), D ** -0.5),
        'moe_b1': nrm((DEPTH, N_EXPERTS, 2 * D_EXPERT), 0.01),
        'moe_w2': nrm((DEPTH, N_EXPERTS, D_EXPERT, D), D_EXPERT ** -0.5),
        'moe_b2': nrm((DEPTH, N_EXPERTS, D), 0.01),
        'final_norm_g': 1.0 + nrm((D,), 0.05),
    }


def reference(x_prompt, x_sample, cache_sb_k, cache_sb_v, state_rwkv, state_rwkv_shift, state_hgrn,
              page_table, c_prompt, c_sample, w_ada, b_ada, norm_attn_g, norm_ffn_g, w_in, sb_bias,
              rw_mu, rw_w0, rw_w_up, rw_a0, rw_a_up, rw_g_up, rw_k_k, rw_k_a, rw_r_k, rw_ln_w, rw_ln_b,
              hg_lb_logits, hg_norm_g, w_branch_sb, w_branch_rw, w_branch_hg, w_out,
              router_w, router_b, moe_w1, moe_b1, moe_w2, moe_b2, final_norm_g):
    prm = dict(w_ada=w_ada, b_ada=b_ada, norm_attn_g=norm_attn_g, norm_ffn_g=norm_ffn_g, w_in=w_in,
               sb_bias=sb_bias,
               rw_mu=rw_mu, rw_w0=rw_w0, rw_w_up=rw_w_up, rw_a0=rw_a0, rw_a_up=rw_a_up, rw_g_up=rw_g_up,
               rw_k_k=rw_k_k, rw_k_a=rw_k_a, rw_r_k=rw_r_k, rw_ln_w=rw_ln_w, rw_ln_b=rw_ln_b,
               hg_lb_logits=hg_lb_logits, hg_norm_g=hg_norm_g, w_branch_sb=w_branch_sb,
               w_branch_rw=w_branch_rw, w_branch_hg=w_branch_hg, w_out=w_out,
               router_w=router_w, router_b=router_b, moe_w1=moe_w1, moe_b1=moe_b1,
               moe_w2=moe_w2, moe_b2=moe_b2, final_norm_g=final_norm_g)

    dt = x_prompt.dtype
    bp = x_prompt.shape[0]
    empty = jnp.zeros((bp, 0, SB_HEADS, HEAD_DIM), dt)

    def prompt_past(l):
        return empty, empty

    y_prompt, k_p, v_p, rw_p, sh_p, hg_p = _run_group(
        x_prompt, c_prompt, prompt_past,
        jnp.zeros((DEPTH, bp, RW_HEADS, HEAD_DIM, HEAD_DIM), dt),
        jnp.zeros((DEPTH, bp, RW_COLS), dt),
        jnp.zeros((DEPTH, bp, HG_HEADS, HEAD_DIM, HEAD_DIM), dt),
        0, prm)

    db = x_sample.shape[0]
    n_pages = page_table.shape[1]
    past_len = n_pages * PAGE_SIZE

    def sample_past(l):
        pk = cache_sb_k[l][page_table].reshape(db, past_len, SB_HEADS, HEAD_DIM)
        pv = cache_sb_v[l][page_table].reshape(db, past_len, SB_HEADS, HEAD_DIM)
        return pk, pv

    y_sample, k_s, v_s, rw_s, sh_s, hg_s = _run_group(
        x_sample, c_sample, sample_past, state_rwkv, state_rwkv_shift, state_hgrn, past_len, prm)

    return (y_prompt, y_sample, k_p, v_p, rw_p, sh_p, hg_p, k_s, v_s, rw_s, sh_s, hg_s)
```

```python
import functools
import math

import jax
import jax.numpy as jnp
from jax import lax
from jax.experimental import pallas as pl
from jax.experimental.pallas import tpu as pltpu

F32 = jnp.float32
BF16 = jnp.bfloat16

HEAD_DIM = 64
PAGE_SIZE = 128
TOP_K = 4
SWIGLU_LIMIT = 7.0
SWIGLU_ALPHA = 1.702
NORM_EPS = 1e-5
RW_GN_EPS = HEAD_DIM * 1e-5
RW_DECAY_LORA = 64
RW_AAA_LORA = 64
RW_GATE_LORA = 160
LANE = 128
VMEM_LIMIT = 48 * 1024 * 1024

NN = (((1,), (0,)), ((), ()))
NT = (((1,), (1,)), ((), ()))
TN = (((0,), (0,)), ((), ()))


def _cparams(sem):
    return pltpu.CompilerParams(dimension_semantics=sem, vmem_limit_bytes=VMEM_LIMIT)


def _split(x, n):
    parts = []
    r = x
    for i in range(n):
        p = r.astype(BF16)
        parts.append(p)
        if i + 1 < n:
            r = r - p.astype(F32)
    return parts


def _mm(a, b, dims=NN, pa=1, pb=1):
    a_parts = _split(a, pa) if a.dtype != BF16 else [a]
    b_parts = _split(b, pb) if b.dtype != BF16 else [b]
    order = max(len(a_parts), len(b_parts))
    out = None
    for i, ap in enumerate(a_parts):
        for j, bp in enumerate(b_parts):
            if i + j < order:
                t = lax.dot_general(ap, bp, dims, preferred_element_type=F32)
                out = t if out is None else out + t
    return out


def _log_sigmoid(z):
    return jnp.minimum(z, 0.0) - jnp.log1p(jnp.exp(-jnp.abs(z)))


def _softplus(z):
    return jnp.maximum(z, 0.0) + jnp.log1p(jnp.exp(-jnp.abs(z)))


def _iota(shape, dim):
    return lax.broadcasted_iota(jnp.int32, shape, dim)


def _divisor(n, target, mult=8):
    if n <= target:
        return n
    for d in range(target, 0, -1):
        if n % d == 0 and d % mult == 0:
            return d
    return n


def _row_tiles(batch, seq, target):
    if seq >= target:
        return 1, _divisor(seq, target)
    bb = 1
    for d in range(1, batch + 1):
        if batch % d == 0 and d * seq <= target:
            bb = d
    return bb, seq


def _ada_kernel(c_ref, w_ref, b_ref, o_ref):
    c = c_ref[...]
    act = c * jax.nn.sigmoid(c)
    o_ref[0] = _mm(act, w_ref[0], pa=2, pb=2) + b_ref[0]


def _ada(c_all, w_ada, b_ada):
    depth, d, n6 = w_ada.shape
    rows = c_all.shape[0]
    tn = _divisor(n6, 1536, LANE)
    return pl.pallas_call(
        _ada_kernel,
        grid=(depth, n6 // tn),
        in_specs=[
            pl.BlockSpec((rows, d), lambda l, j: (0, 0)),
            pl.BlockSpec((1, d, tn), lambda l, j: (l, 0, j)),
            pl.BlockSpec((1, 1, tn), lambda l, j: (l, 0, j)),
        ],
        out_specs=pl.BlockSpec((1, rows, tn), lambda l, j: (l, 0, j)),
        out_shape=jax.ShapeDtypeStruct((depth, rows, n6), F32),
        compiler_params=_cparams(("parallel", "parallel")),
        name="ada",
    )(c_all, w_ada, b_ada.reshape(depth, 1, n6))


def _norm_kernel(*refs, has_delta, emit_x):
    it = iter(refs)
    x_ref = next(it)
    if has_delta:
        d_ref = next(it)
        gt_ref = next(it)
    sh_ref = next(it)
    sc_ref = next(it)
    g_ref = next(it)
    if emit_x:
        xo_ref = next(it)
    h_ref = next(it)
    x = x_ref[...]
    if has_delta:
        x = x + gt_ref[...] * d_ref[...]
    if emit_x:
        xo_ref[...] = x
    ms = jnp.mean(x * x, axis=-1, keepdims=True)
    y = x * lax.rsqrt(ms + NORM_EPS) * g_ref[...]
    h_ref[...] = (y * (1.0 + sc_ref[...]) + sh_ref[...]).astype(h_ref.dtype)


def _norm(x3, delta3, gt, sh, sc, g, *, emit_x, h_dtype):
    b, t, d = x3.shape
    bb, tt = _row_tiles(b, t, 512)
    tile = pl.BlockSpec((bb, tt, d), lambda i, j: (i, j, 0))
    per_seq = pl.BlockSpec((bb, 1, d), lambda i, j: (i, 0, 0))
    has_delta = delta3 is not None
    args, in_specs = [x3], [tile]
    if has_delta:
        args += [delta3, gt]
        in_specs += [tile, per_seq]
    args += [sh, sc, g.reshape(1, 1, d)]
    in_specs += [per_seq, per_seq, pl.BlockSpec((1, 1, d), lambda i, j: (0, 0, 0))]
    out_shape, out_specs = [], []
    if emit_x:
        out_shape.append(jax.ShapeDtypeStruct((b, t, d), F32))
        out_specs.append(tile)
    out_shape.append(jax.ShapeDtypeStruct((b, t, d), h_dtype))
    out_specs.append(tile)
    outs = pl.pallas_call(
        functools.partial(_norm_kernel, has_delta=has_delta, emit_x=emit_x),
        grid=(b // bb, t // tt),
        in_specs=in_specs,
        out_specs=out_specs,
        out_shape=out_shape,
        compiler_params=_cparams(("parallel", "parallel")),
        name="norm",
    )(*args)
    return outs if emit_x else (None, outs[0])


def _proj_kernel(h_ref, w_ref, *o_refs, widths, chunk, passes):
    h = h_ref[...]
    col = 0
    for o_ref, width in zip(o_refs, widths):
        for c0 in range(0, width, chunk):
            c1 = min(c0 + chunk, width)
            o_ref[:, c0:c1] = _mm(h, w_ref[:, col + c0:col + c1], pa=passes, pb=passes)
        col += width


def _proj(h, w, widths, passes):
    n, k = h.shape
    tm = _divisor(n, 512 if passes == 1 else 256)
    return pl.pallas_call(
        functools.partial(_proj_kernel, widths=tuple(widths), chunk=512, passes=passes),
        grid=(n // tm,),
        in_specs=[
            pl.BlockSpec((tm, k), lambda i: (i, 0)),
            pl.BlockSpec((k, w.shape[1]), lambda i: (0, 0)),
        ],
        out_specs=[pl.BlockSpec((tm, wd), lambda i: (i, 0)) for wd in widths],
        out_shape=[jax.ShapeDtypeStruct((n, wd), F32) for wd in widths],
        compiler_params=_cparams(("parallel",)),
        name="proj",
    )(h, w)


def _sb_weights(z, run, upper, readable):
    log_beta = _log_sigmoid(z)
    log_skip = log_beta - z
    if readable is not None:
        log_skip = jnp.where(readable, log_skip, 0.0)
    later = _mm(log_skip, upper, pa=2) + run
    w = jnp.exp(log_beta + later)
    if readable is not None:
        w = jnp.where(readable, w, 0.0)
    return w, run + jnp.sum(log_skip, axis=-1, keepdims=True)


def _sb_prompt_kernel(bias_ref, q_ref, k_ref, v_ref, o_ref, *, tq, n_heads):
    i = pl.program_id(1)
    scale = HEAD_DIM ** -0.5
    row = _iota((tq, tq), 0)
    col = _iota((tq, tq), 1)
    upper = jnp.where(row > col, 1.0, 0.0).astype(BF16)
    diag_readable = col < row
    for h in range(n_heads):
        lanes = slice(h * HEAD_DIM, (h + 1) * HEAD_DIM)
        qh = (q_ref[0, :, lanes] * scale).astype(BF16)
        bias = bias_ref[h]

        def block(j, run, readable):
            start = pl.multiple_of(j * tq, tq)
            kh = k_ref[0, pl.ds(start, tq), lanes].astype(BF16)
            vh = v_ref[0, pl.ds(start, tq), lanes].astype(BF16)
            z = lax.dot_general(qh, kh, NT, preferred_element_type=F32) + bias
            w, run = _sb_weights(z, run, upper, readable)
            return lax.dot_general(w.astype(BF16), vh, NN, preferred_element_type=F32), run

        acc, run = block(i, jnp.zeros((tq, 1), F32), diag_readable)

        def body(jj, carry):
            acc, run = carry
            out, run = block(i - 1 - jj, run, None)
            return acc + out, run

        acc, _ = lax.fori_loop(0, i, body, (acc, run))
        o_ref[0, :, lanes] = acc


def _sb_prompt(q, k, v, bias):
    b, t, width = q.shape
    n_heads = width // HEAD_DIM
    tq = _divisor(t, 256)
    return pl.pallas_call(
        functools.partial(_sb_prompt_kernel, tq=tq, n_heads=n_heads),
        grid=(b, t // tq),
        in_specs=[
            pl.BlockSpec(memory_space=pltpu.SMEM),
            pl.BlockSpec((1, tq, width), lambda bi, i: (bi, i, 0)),
            pl.BlockSpec((1, t, width), lambda bi, i: (bi, 0, 0)),
            pl.BlockSpec((1, t, width), lambda bi, i: (bi, 0, 0)),
        ],
        out_specs=pl.BlockSpec((1, tq, width), lambda bi, i: (bi, i, 0)),
        out_shape=jax.ShapeDtypeStruct((b, t, width), F32),
        compiler_params=_cparams(("parallel", "arbitrary")),
        name="sb_prompt",
    )(bias, q, k, v)


def _sb_sample_kernel(pt_ref, bias_ref, q_ref, kn_ref, vn_ref, *refs, n_pages, n_heads, t_new):
    del pt_ref
    k_refs = refs[:n_pages]
    v_refs = refs[n_pages:2 * n_pages]
    o_ref = refs[2 * n_pages]
    width = n_heads * HEAD_DIM
    cols = n_heads * t_new
    scale = HEAD_DIM ** -0.5
    q = q_ref[0] * scale
    q_exp = jnp.concatenate([q] * n_heads, axis=0)
    head_of_row = _iota((cols, width), 0) // t_new
    head_of_lane = _iota((cols, width), 1) // HEAD_DIM
    q_exp = jnp.where(head_of_row == head_of_lane, q_exp, 0.0).astype(BF16)
    bias_row = jnp.zeros((1, cols), F32)
    col_head = _iota((1, cols), 1) // t_new
    for h in range(n_heads):
        bias_row = jnp.where(col_head == h, bias_ref[h], bias_row)

    def block(k_blk, v_blk, run, readable):
        nk = k_blk.shape[0]
        z = lax.dot_general(k_blk.astype(BF16), q_exp, NT, preferred_element_type=F32) + bias_row
        log_beta = _log_sigmoid(z)
        log_skip = log_beta - z
        if readable is not None:
            log_skip = jnp.where(readable, log_skip, 0.0)
        later_mat = jnp.where(_iota((nk, nk), 1) > _iota((nk, nk), 0), 1.0, 0.0).astype(BF16)
        later = _mm(later_mat, log_skip, pb=2) + run
        w = jnp.exp(log_beta + later)
        if readable is not None:
            w = jnp.where(readable, w, 0.0)
        out = lax.dot_general(w.astype(BF16), v_blk.astype(BF16), TN, preferred_element_type=F32)
        return out, run + jnp.sum(log_skip, axis=0, keepdims=True)

    key_idx = _iota((t_new, cols), 0)
    query_idx = _iota((t_new, cols), 1) % t_new
    acc, run = block(kn_ref[0], vn_ref[0], jnp.zeros((1, cols), F32), key_idx < query_idx)
    for p in range(n_pages - 1, -1, -1):
        out, run = block(k_refs[p][0], v_refs[p][0], run, None)
        acc = acc + out
    lane_head = _iota((t_new, width), 1) // HEAD_DIM
    res = jnp.zeros((t_new, width), F32)
    for h in range(n_heads):
        res = res + jnp.where(lane_head == h, acc[h * t_new:(h + 1) * t_new, :], 0.0)
    o_ref[0] = res


def _sb_sample(q, k_new, v_new, cache_k, cache_v, page_table, bias, layer, n_pool):
    b, t_new, width = q.shape
    n_pages = page_table.shape[1]
    n_heads = width // HEAD_DIM
    base = layer * n_pool

    def page_spec(p):
        return pl.BlockSpec((1, PAGE_SIZE, width), lambda bi, pt, p=p: (base + pt[bi * n_pages + p], 0, 0))

    new_spec = pl.BlockSpec((1, t_new, width), lambda bi, pt: (bi, 0, 0))
    grid_spec = pltpu.PrefetchScalarGridSpec(
        num_scalar_prefetch=1,
        grid=(b,),
        in_specs=[pl.BlockSpec(memory_space=pltpu.SMEM), new_spec, new_spec, new_spec]
        + [page_spec(p) for p in range(n_pages)] + [page_spec(p) for p in range(n_pages)],
        out_specs=new_spec,
    )
    return pl.pallas_call(
        functools.partial(_sb_sample_kernel, n_pages=n_pages, n_heads=n_heads, t_new=t_new),
        grid_spec=grid_spec,
        out_shape=jax.ShapeDtypeStruct((b, t_new, width), F32),
        compiler_params=_cparams(("arbitrary",)),
        name="sb_sample",
    )(page_table.reshape(-1), bias, q, k_new, v_new, *([cache_k] * n_pages), *([cache_v] * n_pages))


def _unit_lower_inverse(n_mat, size):
    row = _iota((size, size), 0)
    col = _iota((size, size), 1)
    eye = jnp.where(row == col, 1.0, 0.0)
    inv = eye + jnp.where(row // 2 == col // 2, n_mat, 0.0)
    s = 2
    while s < size:
        off = jnp.where((row // (2 * s) == col // (2 * s)) & (row // s != col // s), n_mat, 0.0)
        inv = inv + _mm(_mm(inv, off, pa=2, pb=2), inv, pa=2, pb=2)
        s *= 2
    return inv


def _rwkv_kernel(p_ref, shift_ref, s0_ref, mu_ref, w0_ref, wup_ref, a0_ref, aup_ref, gup_ref, kk_ref, ka_ref,
                 rk_ref, lnw_ref, lnb_ref, o_ref, s_ref, state, prev, *, chunk, n_heads, passes):
    c = pl.program_id(1)
    width = n_heads * HEAD_DIM
    mm = functools.partial(_mm, pa=passes, pb=passes)

    @pl.when(c == 0)
    def _():
        state[...] = s0_ref[0]
        prev[...] = shift_ref[0]

    p = p_ref[0]
    rolled = pltpu.roll(p, 1, axis=0)
    prev_rows = jnp.where(_iota(p.shape, 0) == 0, prev[...], rolled)
    prev[...] = p[chunk - 1:chunk, :]
    xs = p + (prev_rows - p) * mu_ref[...]
    r = xs[:, 0:width]
    k = xs[:, width:2 * width]
    v = xs[:, 2 * width:3 * width]
    o1 = 3 * width
    wd = xs[:, o1:o1 + LANE]
    ad = xs[:, o1 + LANE:o1 + 2 * LANE]
    gd = xs[:, o1 + 2 * LANE:o1 + 2 * LANE + 2 * LANE]
    w_log = -_softplus(-(w0_ref[...] + mm(jnp.tanh(wd), wup_ref[...]))) - 0.5
    log_decay = -jnp.exp(w_log)
    a = jax.nn.sigmoid(a0_ref[...] + mm(ad, aup_ref[...]))
    g = mm(jax.nn.sigmoid(gd), gup_ref[...])
    kk_all = k * kk_ref[...]
    k2 = k * (1.0 + (a - 1.0) * ka_ref[...])

    row = _iota((chunk, chunk), 0)
    col = _iota((chunk, chunk), 1)
    incl = jnp.where(col <= row, 1.0, 0.0).astype(BF16)
    b_all = _mm(incl, log_decay, pb=3)
    strict = col < row
    lower = col <= row

    for h in range(n_heads):
        lanes = slice(h * HEAD_DIM, (h + 1) * HEAD_DIM)
        kkh = kk_all[:, lanes]
        kkh = kkh / jnp.maximum(jnp.sqrt(jnp.sum(kkh * kkh, axis=-1, keepdims=True)), 1e-12)
        bh = b_all[:, lanes]
        rh, kh, vh, ah = r[:, lanes], k2[:, lanes], v[:, lanes], a[:, lanes]
        eb = jnp.exp(bh)
        einv = jnp.exp(-bh)
        al = -kkh * jnp.exp(bh - log_decay[:, lanes])
        be = kkh * ah * einv
        kb = kh * einv
        rb = rh * eb
        s0 = state[h]
        n_mat = jnp.where(strict, mm(al, be, NT), 0.0)
        a_k = jnp.where(strict, mm(al, kb, NT), 0.0)
        t_inv = _unit_lower_inverse(n_mat, chunk)
        u = _mm(t_inv, mm(al, s0, NT) + mm(a_k, vh), pa=2, pb=2)
        y = (mm(rb, s0, NT) + mm(jnp.where(lower, mm(rb, be, NT), 0.0), u)
             + mm(jnp.where(lower, mm(rb, kb, NT), 0.0), vh))
        state[h] = (s0 + mm(u, be, TN) + mm(vh, kb, TN)) * eb[chunk - 1:chunk, :]
        mean = jnp.mean(y, axis=-1, keepdims=True)
        var = jnp.mean(jnp.square(y - mean), axis=-1, keepdims=True)
        yn = (y - mean) * lax.rsqrt(var + RW_GN_EPS) * lnw_ref[:, lanes] + lnb_ref[:, lanes]
        bonus = jnp.sum(rh * kh * rk_ref[:, lanes], axis=-1, keepdims=True) * vh
        o_ref[0, :, lanes] = (yn + bonus) * g[:, lanes]

    s_ref[0] = state[...]


def _rwkv(p_rw, shift, s0, prm, passes):
    b, t, cols = p_rw.shape
    n_heads = s0.shape[1]
    width = n_heads * HEAD_DIM
    chunk = _divisor(t, 64)
    vec = lambda n: pl.BlockSpec((1, n), lambda bi, ci: (0, 0))
    mat = lambda m, n: pl.BlockSpec((m, n), lambda bi, ci: (0, 0))
    return pl.pallas_call(
        functools.partial(_rwkv_kernel, chunk=chunk, n_heads=n_heads, passes=passes),
        grid=(b, t // chunk),
        in_specs=[
            pl.BlockSpec((1, chunk, cols), lambda bi, ci: (bi, ci, 0)),
            pl.BlockSpec((1, 1, cols), lambda bi, ci: (bi, 0, 0)),
            pl.BlockSpec((1, n_heads, HEAD_DIM, HEAD_DIM), lambda bi, ci: (bi, 0, 0, 0)),
            vec(cols), vec(width), mat(LANE, width), vec(width), mat(LANE, width), mat(2 * LANE, width),
            vec(width), vec(width), vec(width), vec(width), vec(width),
        ],
        out_specs=[
            pl.BlockSpec((1, chunk, width), lambda bi, ci: (bi, ci, 0)),
            pl.BlockSpec((1, n_heads, HEAD_DIM, HEAD_DIM), lambda bi, ci: (bi, 0, 0, 0)),
        ],
        out_shape=[
            jax.ShapeDtypeStruct((b, t, width), F32),
            jax.ShapeDtypeStruct((b, n_heads, HEAD_DIM, HEAD_DIM), F32),
        ],
        scratch_shapes=[pltpu.VMEM((n_heads, HEAD_DIM, HEAD_DIM), F32), pltpu.VMEM((1, cols), F32)],
        compiler_params=_cparams(("parallel", "arbitrary")),
        name="rwkv7",
    )(p_rw, shift, s0, prm["mu"], prm["w0"], prm["w_up"], prm["a0"], prm["a_up"], prm["g_up"],
      prm["k_k"], prm["k_a"], prm["r_k"], prm["ln_w"], prm["ln_b"])


def _hgrn_kernel(p_ref, s0_ref, lbl_ref, ng_ref, o_ref, s_ref, st, *, chunk, n_heads, layer, passes):
    c = pl.program_id(1)
    width = n_heads * HEAD_DIM
    mm = functools.partial(_mm, pa=passes, pb=passes)
    head_r = _iota((width, width), 0) // HEAD_DIM
    head_c = _iota((width, width), 1) // HEAD_DIM
    same_head = head_r == head_c
    ones_bd = jnp.where(same_head, 1.0, 0.0).astype(BF16)

    @pl.when(c == 0)
    def _():
        st[...] = jnp.zeros_like(st)
        for h in range(n_heads):
            lanes = slice(h * HEAD_DIM, (h + 1) * HEAD_DIM)
            st[lanes, lanes] = s0_ref[0, h].T

    logits = lbl_ref[...]
    e = jnp.exp(logits - jnp.max(logits, axis=0, keepdims=True))
    sm = e / jnp.sum(e, axis=0, keepdims=True)
    lb = jnp.zeros((1, width), F32)
    for j in range(1, layer + 1):
        lb = lb + sm[j:j + 1, :]

    p = p_ref[0]
    fp = p[:, 0:width]
    iv = p[:, width:2 * width]
    q = p[:, 2 * width:3 * width]
    go = p[:, 3 * width:4 * width]
    x1 = jnp.log(lb)
    x2 = jnp.log1p(-lb) + _log_sigmoid(fp)
    log_f = jnp.maximum(x1, x2) + jnp.log1p(jnp.exp(-jnp.abs(x1 - x2)))
    key = (1.0 - lb) * jax.nn.sigmoid(-fp)

    row = _iota((chunk, chunk), 0)
    col = _iota((chunk, chunk), 1)
    incl = jnp.where(col <= row, 1.0, 0.0).astype(BF16)
    b = _mm(incl, log_f, pb=3)
    b_end = b[chunk - 1:chunk, :]

    st0 = st[...]
    inter = mm(q * jnp.exp(b), st0, NT)
    diff = b[None, :, :] - b[:, None, :]
    s_idx = _iota((chunk, chunk, width), 0)
    t_idx = _iota((chunk, chunk, width), 1)
    pair = jnp.where(s_idx <= t_idx, jnp.exp(jnp.minimum(diff, 0.0)) * q[None, :, :] * key[:, None, :], 0.0)
    att = _mm(pair.reshape(chunk * chunk, width), ones_bd, pa=2).reshape(chunk, chunk, width)
    intra = jnp.sum(att * iv[:, None, :], axis=0)
    st[...] = st0 * jnp.exp(b_end) + jnp.where(same_head, mm(iv, key * jnp.exp(b_end - b), TN), 0.0)

    o = inter + intra
    ms = _mm(o * o, ones_bd, pa=2) * (1.0 / HEAD_DIM)
    o_ref[0] = o * lax.rsqrt(ms + NORM_EPS) * ng_ref[...] * jax.nn.sigmoid(go)

    @pl.when(c == pl.num_programs(1) - 1)
    def _():
        for h in range(n_heads):
            lanes = slice(h * HEAD_DIM, (h + 1) * HEAD_DIM)
            s_ref[0, h] = st[lanes, lanes].T


def _hgrn(p_hg, s0, lb_logits, norm_g_tiled, layer, passes):
    b, t, cols = p_hg.shape
    n_heads = s0.shape[1]
    width = n_heads * HEAD_DIM
    chunk = _divisor(math.gcd(t, 64), 16)
    depth = lb_logits.shape[0]
    return pl.pallas_call(
        functools.partial(_hgrn_kernel, chunk=chunk, n_heads=n_heads, layer=layer, passes=passes),
        grid=(b, t // chunk),
        in_specs=[
            pl.BlockSpec((1, chunk, cols), lambda bi, ci: (bi, ci, 0)),
            pl.BlockSpec((1, n_heads, HEAD_DIM, HEAD_DIM), lambda bi, ci: (bi, 0, 0, 0)),
            pl.BlockSpec((depth, width), lambda bi, ci: (0, 0)),
            pl.BlockSpec((1, width), lambda bi, ci: (0, 0)),
        ],
        out_specs=[
            pl.BlockSpec((1, chunk, width), lambda bi, ci: (bi, ci, 0)),
            pl.BlockSpec((1, n_heads, HEAD_DIM, HEAD_DIM), lambda bi, ci: (bi, 0, 0, 0)),
        ],
        out_shape=[
            jax.ShapeDtypeStruct((b, t, width), F32),
            jax.ShapeDtypeStruct((b, n_heads, HEAD_DIM, HEAD_DIM), F32),
        ],
        scratch_shapes=[pltpu.VMEM((width, width), F32)],
        compiler_params=_cparams(("parallel", "arbitrary")),
        name="hgrn2",
    )(p_hg, s0, lb_logits, norm_g_tiled)


def _merge_kernel(x_ref, gt_ref, sh_ref, sc_ref, osb_ref, orw_ref, ohg_ref, pg_ref, wsb_ref, wrw_ref, whg_ref,
                  wout_ref, ng_ref, rw_ref, rb_ref, xo_ref, h_ref, gates_ref, *, rows, d, passes):
    mm = functools.partial(_mm, pa=passes, pb=passes)

    def flat(ref):
        val = ref[...]
        return val.reshape(rows, val.shape[-1])

    gates = jax.nn.sigmoid(flat(pg_ref))
    merged = (gates[:, 0:d] * mm(flat(osb_ref), wsb_ref[...])
              + gates[:, d:2 * d] * mm(flat(orw_ref), wrw_ref[...])
              + gates[:, 2 * d:3 * d] * mm(flat(ohg_ref), whg_ref[...]))
    upd = mm(merged, wout_ref[...]).reshape(x_ref.shape)
    x = x_ref[...] + gt_ref[...] * upd
    xo_ref[...] = x
    ms = jnp.mean(x * x, axis=-1, keepdims=True)
    y = x * lax.rsqrt(ms + NORM_EPS) * ng_ref[...]
    h3 = y * (1.0 + sc_ref[...]) + sh_ref[...]
    h_ref[...] = h3.astype(h_ref.dtype)
    h = h3.reshape(rows, d)
    logits = _mm(h, rw_ref[...], pa=2, pb=2) + rb_ref[...]
    n_exp = logits.shape[-1]
    lane = _iota(logits.shape, 1)
    work = logits
    picks, vals = [], []
    for _ in range(TOP_K):
        m = jnp.max(work, axis=-1, keepdims=True)
        idx = jnp.min(jnp.where(work == m, lane, n_exp), axis=-1, keepdims=True)
        hit = lane == idx
        picks.append(hit)
        vals.append(m)
        work = jnp.where(hit, -jnp.inf, work)
    exps = [jnp.exp(vv - vals[0]) for vv in vals]
    denom = exps[0]
    for ee in exps[1:]:
        denom = denom + ee
    out = jnp.zeros_like(logits)
    for hit, ee in zip(picks, exps):
        out = out + jnp.where(hit, ee / denom, 0.0)
    gates_ref[...] = out.reshape(gates_ref.shape)


def _merge(x3, gt, sh, sc, o_sb, o_rw, o_hg, p_gate, w_sb, w_rw, w_hg, w_out, norm_g, router_w, router_b,
           passes):
    b, t, d = x3.shape
    bb, tt = _row_tiles(b, t, 256)
    n_exp = router_w.shape[1]
    tile = lambda w: pl.BlockSpec((bb, tt, w), lambda i, j: (i, j, 0))
    per_seq = pl.BlockSpec((bb, 1, d), lambda i, j: (i, 0, 0))
    whole = lambda a: pl.BlockSpec(a.shape, lambda i, j: (0,) * a.ndim)
    ng = norm_g.reshape(1, 1, d)
    rb = router_b.reshape(1, n_exp)
    return pl.pallas_call(
        functools.partial(_merge_kernel, rows=bb * tt, d=d, passes=passes),
        grid=(b // bb, t // tt),
        in_specs=[tile(d), per_seq, per_seq, per_seq, tile(o_sb.shape[-1]), tile(o_rw.shape[-1]),
                  tile(o_hg.shape[-1]), tile(p_gate.shape[-1]), whole(w_sb), whole(w_rw), whole(w_hg),
                  whole(w_out), whole(ng), whole(router_w), whole(rb)],
        out_specs=[tile(d), tile(d), tile(n_exp)],
        out_shape=[jax.ShapeDtypeStruct((b, t, d), F32), jax.ShapeDtypeStruct((b, t, d), BF16),
                   jax.ShapeDtypeStruct((b, t, n_exp), F32)],
        compiler_params=_cparams(("parallel", "parallel")),
        name="merge",
    )(x3, gt, sh, sc, o_sb, o_rw, o_hg, p_gate, w_sb, w_rw, w_hg, w_out, ng, router_w, rb)


def _moe_kernel(h_ref, g_ref, w1_ref, b1_ref, w2_ref, b2_ref, o_ref, *, d_exp):
    e = pl.program_id(1)

    @pl.when(e == 0)
    def _():
        o_ref[...] = jnp.zeros_like(o_ref)

    gu = lax.dot_general(h_ref[...], w1_ref[0], NN, preferred_element_type=F32) + b1_ref[0]
    glu = jnp.minimum(gu[:, :d_exp], SWIGLU_LIMIT)
    lin = jnp.clip(gu[:, d_exp:], -SWIGLU_LIMIT, SWIGLU_LIMIT)
    act = glu * jax.nn.sigmoid(SWIGLU_ALPHA * glu) * (lin + 1.0)
    y = lax.dot_general(act.astype(BF16), w2_ref[0], NN, preferred_element_type=F32) + b2_ref[0]
    gates = g_ref[...]
    ge = jnp.sum(jnp.where(_iota(gates.shape, 1) == e, gates, 0.0), axis=-1, keepdims=True)
    o_ref[...] += ge * y


def _moe(h, gates, w1, b1, w2, b2):
    n, d = h.shape
    n_exp, _, two_de = w1.shape
    d_exp = two_de // 2
    tm = _divisor(n, 1024)
    return pl.pallas_call(
        functools.partial(_moe_kernel, d_exp=d_exp),
        grid=(n // tm, n_exp),
        in_specs=[
            pl.BlockSpec((tm, d), lambda i, e: (i, 0)),
            pl.BlockSpec((tm, n_exp), lambda i, e: (i, 0)),
            pl.BlockSpec((1, d, two_de), lambda i, e: (e, 0, 0)),
            pl.BlockSpec((1, 1, two_de), lambda i, e: (e, 0, 0)),
            pl.BlockSpec((1, d_exp, d), lambda i, e: (e, 0, 0)),
            pl.BlockSpec((1, 1, d), lambda i, e: (e, 0, 0)),
        ],
        out_specs=pl.BlockSpec((tm, d), lambda i, e: (i, 0)),
        out_shape=jax.ShapeDtypeStruct((n, d), F32),
        compiler_params=_cparams(("parallel", "arbitrary")),
        name="moe",
    )(h, gates, w1, b1.reshape(n_exp, 1, two_de), w2, b2.reshape(n_exp, 1, d))


def _pad_cols(a, n):
    return jnp.pad(a, [(0, 0)] * (a.ndim - 1) + [(0, n - a.shape[-1])])


def _pad_rows(a, n):
    return jnp.pad(a, [(0, n - a.shape[0]), (0, 0)])


def _rw_layout(a, rw_width):
    o1 = 3 * rw_width
    o2 = o1 + RW_DECAY_LORA
    o3 = o2 + RW_AAA_LORA
    return jnp.concatenate([
        a[..., :o1], _pad_cols(a[..., o1:o2], LANE), _pad_cols(a[..., o2:o3], LANE),
        _pad_cols(a[..., o3:], 2 * LANE)], axis=-1)


def _rw_unlayout(a, rw_width):
    o1 = 3 * rw_width
    return jnp.concatenate([
        a[..., :o1], a[..., o1:o1 + RW_DECAY_LORA], a[..., o1 + LANE:o1 + LANE + RW_AAA_LORA],
        a[..., o1 + 2 * LANE:o1 + 2 * LANE + RW_GATE_LORA]], axis=-1)


def _run_group(x, mod, read_past, rw_state, rw_shift, hg_state, lw, depth):
    b, t, d = x.shape
    sb_w = lw[0]["sb_w"]
    rw_w = lw[0]["rw_w"]
    delta, gt_prev = None, None
    ks, vs, rws, shs, hgs = [], [], [], [], []
    for l in range(depth):
        w = lw[l]
        m = mod[l].reshape(b, 1, 6 * d)
        sh_a, sc_a, gt_a, sh_f, sc_f, gt_f = [m[:, :, i * d:(i + 1) * d] for i in range(6)]
        passes = w["passes"]
        x_new, h = _norm(x, delta, gt_prev, sh_a, sc_a, w["norm_attn_g"], emit_x=delta is not None,
                         h_dtype=BF16 if passes == 1 else F32)
        if x_new is not None:
            x = x_new
        hf = h.reshape(b * t, d)
        q, k, v = [a.reshape(b, t, sb_w) for a in _proj(hf, w["w_sb"], (sb_w, sb_w, sb_w), passes)]
        (p_rw,) = _proj(hf, w["w_rw"], (w["w_rw"].shape[1],), passes)
        (p_hg,) = _proj(hf, w["w_hg"], (w["w_hg"].shape[1],), passes)
        (p_gate,) = _proj(hf, w["w_gate"], (w["w_gate"].shape[1],), passes)
        p_rw = p_rw.reshape(b, t, -1)
        p_hg = p_hg.reshape(b, t, -1)
        p_gate = p_gate.reshape(b, t, -1)
        o_sb = read_past(l, q, k, v, w["sb_bias"])
        o_rw, s_rw = _rwkv(p_rw, _rw_layout(rw_shift[l], rw_w).reshape(b, 1, -1), rw_state[l], w["rw"], passes)
        o_hg, s_hg = _hgrn(p_hg, hg_state[l], w["hg_lb_logits"], w["hg_norm_g"], l, passes)
        x, h2, gates = _merge(x, gt_a, sh_f, sc_f, o_sb, o_rw, o_hg, p_gate, w["w_branch_sb"], w["w_branch_rw"],
                              w["w_branch_hg"], w["w_out"], w["norm_ffn_g"], w["router_w"], w["router_b"], passes)
        delta = _moe(h2.reshape(b * t, d), gates.reshape(b * t, -1), w["moe_w1"], w["moe_b1"], w["moe_w2"],
                     w["moe_b2"]).reshape(b, t, d)
        gt_prev = gt_f
        ks.append(k.reshape(b, t, sb_w // HEAD_DIM, HEAD_DIM))
        vs.append(v.reshape(b, t, sb_w // HEAD_DIM, HEAD_DIM))
        rws.append(s_rw)
        shs.append(_rw_unlayout(p_rw[:, -1], rw_w))
        hgs.append(s_hg)
    zeros = jnp.zeros((b, 1, d), F32)
    _, y = _norm(x, delta, gt_prev, zeros, zeros, lw[0]["final_norm_g"], emit_x=False, h_dtype=F32)
    return y, jnp.stack(ks), jnp.stack(vs), jnp.stack(rws), jnp.stack(shs), jnp.stack(hgs)


def kernel(x_prompt, x_sample, cache_sb_k, cache_sb_v, state_rwkv, state_rwkv_shift, state_hgrn, page_table,
           c_prompt, c_sample, w_ada, b_ada, norm_attn_g, norm_ffn_g, w_in, sb_bias, rw_mu, rw_w0, rw_w_up, rw_a0,
           rw_a_up, rw_g_up, rw_k_k, rw_k_a, rw_r_k, rw_ln_w, rw_ln_b, hg_lb_logits, hg_norm_g, w_branch_sb,
           w_branch_rw, w_branch_hg, w_out, router_w, router_b, moe_w1, moe_b1, moe_w2, moe_b2, final_norm_g):
    depth, d, _ = w_in.shape
    bp = x_prompt.shape[0]
    db = x_sample.shape[0]
    sb_w = w_branch_sb.shape[1]
    rw_w = w_branch_rw.shape[1]
    hg_w = w_branch_hg.shape[1]
    rw_heads = rw_w // HEAD_DIM
    hg_heads = hg_w // HEAD_DIM
    rw_cols = 3 * rw_w + RW_DECAY_LORA + RW_AAA_LORA + RW_GATE_LORA
    o_rw = 3 * sb_w
    o_hg = o_rw + rw_cols
    o_gate = o_hg + 4 * hg_w

    mod = _ada(jnp.concatenate([c_prompt, c_sample], axis=0), w_ada, b_ada)

    lw = []
    for l in range(depth):
        wl = w_in[l]
        passes = 2 if l + 1 < depth else 1
        wdt = BF16 if passes == 1 else F32
        lw.append(dict(
            sb_w=sb_w, rw_w=rw_w, passes=passes,
            w_sb=wl[:, :o_rw].astype(wdt),
            w_rw=_rw_layout(wl[:, o_rw:o_hg], rw_w).astype(wdt),
            w_hg=wl[:, o_hg:o_gate].astype(wdt),
            w_gate=wl[:, o_gate:].astype(wdt),
            norm_attn_g=norm_attn_g[l], norm_ffn_g=norm_ffn_g[l], final_norm_g=final_norm_g,
            sb_bias=sb_bias[l],
            rw=dict(
                mu=_rw_layout(rw_mu[l][None, :], rw_w), w0=rw_w0[l][None, :],
                w_up=_pad_rows(rw_w_up[l], LANE), a0=rw_a0[l][None, :], a_up=_pad_rows(rw_a_up[l], LANE),
                g_up=_pad_rows(rw_g_up[l], 2 * LANE), k_k=rw_k_k[l][None, :], k_a=rw_k_a[l][None, :],
                r_k=rw_r_k[l].reshape(1, rw_w), ln_w=rw_ln_w[l][None, :], ln_b=rw_ln_b[l][None, :]),
            hg_lb_logits=hg_lb_logits,
            hg_norm_g=jnp.tile(hg_norm_g[l], hg_heads)[None, :],
            w_branch_sb=w_branch_sb[l].astype(wdt), w_branch_rw=w_branch_rw[l].astype(wdt),
            w_branch_hg=w_branch_hg[l].astype(wdt), w_out=w_out[l].astype(wdt),
            router_w=router_w[l], router_b=router_b[l],
            moe_w1=moe_w1[l].astype(BF16), moe_b1=moe_b1[l], moe_w2=moe_w2[l].astype(BF16), moe_b2=moe_b2[l],
        ))

    def prompt_attn(l, q, k, v, bias):
        del l
        return _sb_prompt(q, k, v, bias)

    n_pool = cache_sb_k.shape[1]
    cache_k = cache_sb_k.reshape(depth * n_pool, PAGE_SIZE, sb_w)
    cache_v = cache_sb_v.reshape(depth * n_pool, PAGE_SIZE, sb_w)

    def sample_attn(l, q, k, v, bias):
        return _sb_sample(q, k, v, cache_k, cache_v, page_table, bias, l, n_pool)

    zeros_rw = jnp.zeros((depth, bp, rw_heads, HEAD_DIM, HEAD_DIM), F32)
    zeros_sh = jnp.zeros((depth, bp, rw_cols), F32)
    zeros_hg = jnp.zeros((depth, bp, hg_heads, HEAD_DIM, HEAD_DIM), F32)
    out_p = _run_group(x_prompt, mod[:, :bp], prompt_attn, zeros_rw, zeros_sh, zeros_hg, lw, depth)
    out_s = _run_group(x_sample, mod[:, bp:], sample_attn, state_rwkv, state_rwkv_shift, state_hgrn, lw, depth)
    return (out_p[0], out_s[0]) + tuple(out_p[1:]) + tuple(out_s[1:])
```

```python
import functools
import math

import jax
import jax.numpy as jnp
from jax import lax
from jax.experimental import pallas as pl
from jax.experimental.pallas import tpu as pltpu

F32 = jnp.float32
BF16 = jnp.bfloat16

HEAD_DIM = 64
PAGE_SIZE = 128
TOP_K = 4
SWIGLU_LIMIT = 7.0
SWIGLU_ALPHA = 1.702
NORM_EPS = 1e-5
RW_GN_EPS = HEAD_DIM * 1e-5
RW_DECAY_LORA = 64
RW_AAA_LORA = 64
RW_GATE_LORA = 160
LANE = 128
VMEM_LIMIT = 48 * 1024 * 1024

NN = (((1,), (0,)), ((), ()))
NT = (((1,), (1,)), ((), ()))
TN = (((0,), (0,)), ((), ()))


def _cparams(sem):
    return pltpu.CompilerParams(dimension_semantics=sem, vmem_limit_bytes=VMEM_LIMIT)


def _split(x, n):
    parts = []
    r = x
    for i in range(n):
        p = r.astype(BF16)
        parts.append(p)
        if i + 1 < n:
            r = r - p.astype(F32)
    return parts


MXU_DEPTH = 256


def _mm(a, b, dims=NN, pa=1, pb=1):
    a_parts = _split(a, pa) if a.dtype != BF16 else [a]
    b_parts = _split(b, pb) if b.dtype != BF16 else [b]
    order = max(len(a_parts), len(b_parts))
    terms = [(ap, bp) for i, ap in enumerate(a_parts) for j, bp in enumerate(b_parts) if i + j < order]
    (ca,), (cb,) = dims[0]
    if len(terms) > 1 and len(terms) * a.shape[ca] <= MXU_DEPTH:
        a_cat = jnp.concatenate([t[0] for t in terms], axis=ca)
        b_cat = jnp.concatenate([t[1] for t in terms], axis=cb)
        return lax.dot_general(a_cat, b_cat, dims, preferred_element_type=F32)
    out = None
    for ap, bp in terms:
        t = lax.dot_general(ap, bp, dims, preferred_element_type=F32)
        out = t if out is None else out + t
    return out


def _log_sigmoid(z):
    return jnp.minimum(z, 0.0) - jnp.log1p(jnp.exp(-jnp.abs(z)))


def _sb_logs(z):
    soft = jnp.log(1.0 + jnp.exp(-jnp.abs(z)))
    return jnp.minimum(z, 0.0) - soft, -jnp.maximum(z, 0.0) - soft


def _softplus(z):
    return jnp.maximum(z, 0.0) + jnp.log1p(jnp.exp(-jnp.abs(z)))


def _iota(shape, dim):
    return lax.broadcasted_iota(jnp.int32, shape, dim)


def _divisor(n, target, mult=8):
    if n <= target:
        return n
    for d in range(target, 0, -1):
        if n % d == 0 and d % mult == 0:
            return d
    return n


def _row_tiles(batch, seq, target):
    if seq >= target:
        return 1, _divisor(seq, target)
    bb = 1
    for d in range(1, batch + 1):
        if batch % d == 0 and d * seq <= target:
            bb = d
    return bb, seq


def _ada_kernel(c_ref, w_ref, b_ref, o_ref):
    c = c_ref[...]
    act = c * jax.nn.sigmoid(c)
    o_ref[0] = _mm(act, w_ref[0], pa=2, pb=2) + b_ref[0]


def _ada(c_all, w_ada, b_ada):
    depth, d, n6 = w_ada.shape
    rows = c_all.shape[0]
    tn = _divisor(n6, 1536, LANE)
    return pl.pallas_call(
        _ada_kernel,
        grid=(depth, n6 // tn),
        in_specs=[
            pl.BlockSpec((rows, d), lambda l, j: (0, 0)),
            pl.BlockSpec((1, d, tn), lambda l, j: (l, 0, j)),
            pl.BlockSpec((1, 1, tn), lambda l, j: (l, 0, j)),
        ],
        out_specs=pl.BlockSpec((1, rows, tn), lambda l, j: (l, 0, j)),
        out_shape=jax.ShapeDtypeStruct((depth, rows, n6), F32),
        compiler_params=_cparams(("parallel", "parallel")),
        name="ada",
    )(c_all, w_ada, b_ada.reshape(depth, 1, n6))


def _norm_kernel(*refs, has_delta, emit_x):
    it = iter(refs)
    x_ref = next(it)
    if has_delta:
        d_ref = next(it)
        gt_ref = next(it)
    sh_ref = next(it)
    sc_ref = next(it)
    g_ref = next(it)
    if emit_x:
        xo_ref = next(it)
    h_ref = next(it)
    x = x_ref[...]
    if has_delta:
        x = x + gt_ref[...] * d_ref[...]
    if emit_x:
        xo_ref[...] = x
    ms = jnp.mean(x * x, axis=-1, keepdims=True)
    y = x * lax.rsqrt(ms + NORM_EPS) * g_ref[...]
    h_ref[...] = (y * (1.0 + sc_ref[...]) + sh_ref[...]).astype(h_ref.dtype)


def _norm(x3, delta3, gt, sh, sc, g, *, emit_x, h_dtype):
    b, t, d = x3.shape
    bb, tt = _row_tiles(b, t, 512)
    tile = pl.BlockSpec((bb, tt, d), lambda i, j: (i, j, 0))
    per_seq = pl.BlockSpec((bb, 1, d), lambda i, j: (i, 0, 0))
    has_delta = delta3 is not None
    args, in_specs = [x3], [tile]
    if has_delta:
        args += [delta3, gt]
        in_specs += [tile, per_seq]
    args += [sh, sc, g.reshape(1, 1, d)]
    in_specs += [per_seq, per_seq, pl.BlockSpec((1, 1, d), lambda i, j: (0, 0, 0))]
    out_shape, out_specs = [], []
    if emit_x:
        out_shape.append(jax.ShapeDtypeStruct((b, t, d), F32))
        out_specs.append(tile)
    out_shape.append(jax.ShapeDtypeStruct((b, t, d), h_dtype))
    out_specs.append(tile)
    outs = pl.pallas_call(
        functools.partial(_norm_kernel, has_delta=has_delta, emit_x=emit_x),
        grid=(b // bb, t // tt),
        in_specs=in_specs,
        out_specs=out_specs,
        out_shape=out_shape,
        compiler_params=_cparams(("parallel", "parallel")),
        name="norm",
    )(*args)
    return outs if emit_x else (None, outs[0])


def _proj_kernel(h_ref, w_ref, *o_refs, widths, chunk, passes):
    h = h_ref[...]
    col = 0
    for o_ref, width in zip(o_refs, widths):
        for c0 in range(0, width, chunk):
            c1 = min(c0 + chunk, width)
            o_ref[:, c0:c1] = _mm(h, w_ref[:, col + c0:col + c1], pa=passes, pb=passes)
        col += width


def _proj(h, w, widths, passes):
    n, k = h.shape
    tm = _divisor(n, 512 if passes == 1 else 256)
    return pl.pallas_call(
        functools.partial(_proj_kernel, widths=tuple(widths), chunk=512, passes=passes),
        grid=(n // tm,),
        in_specs=[
            pl.BlockSpec((tm, k), lambda i: (i, 0)),
            pl.BlockSpec((k, w.shape[1]), lambda i: (0, 0)),
        ],
        out_specs=[pl.BlockSpec((tm, wd), lambda i: (i, 0)) for wd in widths],
        out_shape=[jax.ShapeDtypeStruct((n, wd), F32) for wd in widths],
        compiler_params=_cparams(("parallel",)),
        name="proj",
    )(h, w)


def _sb_weights(z, run, upper, readable):
    log_beta, log_skip = _sb_logs(z)
    if readable is not None:
        log_skip = jnp.where(readable, log_skip, 0.0)
    later = _mm(log_skip, upper, pa=2) + run
    w = jnp.exp(log_beta + later)
    if readable is not None:
        w = jnp.where(readable, w, 0.0)
    return w, run + jnp.sum(log_skip, axis=-1, keepdims=True)


SB_HEADS_PER_LOOP = 4
SB_BLOCK = 256


def _sb_prompt_kernel(bias_ref, q_ref, k_ref, v_ref, o_ref, *, tq, n_heads):
    i = pl.program_id(1)
    scale = HEAD_DIM ** -0.5
    row = _iota((tq, tq), 0)
    col = _iota((tq, tq), 1)
    upper = jnp.where(row > col, 1.0, 0.0).astype(BF16)
    diag_readable = col < row
    for h0 in range(0, n_heads, SB_HEADS_PER_LOOP):
        group = list(range(h0, min(h0 + SB_HEADS_PER_LOOP, n_heads)))
        lanes = [slice(h * HEAD_DIM, (h + 1) * HEAD_DIM) for h in group]
        qs = [(q_ref[0, :, ln] * scale).astype(BF16) for ln in lanes]
        biases = [bias_ref[h] for h in group]

        def block(j, g, run, readable):
            start = pl.multiple_of(j * tq, tq)
            kh = k_ref[0, pl.ds(start, tq), lanes[g]].astype(BF16)
            vh = v_ref[0, pl.ds(start, tq), lanes[g]].astype(BF16)
            z = lax.dot_general(qs[g], kh, NT, preferred_element_type=F32) + biases[g]
            w, run = _sb_weights(z, run, upper, readable)
            return lax.dot_general(w.astype(BF16), vh, NN, preferred_element_type=F32), run

        carry = []
        for g in range(len(group)):
            carry.extend(block(i, g, jnp.zeros((tq, 1), F32), diag_readable))

        def body(jj, carry):
            new = []
            for g in range(len(group)):
                out, run = block(i - 1 - jj, g, carry[2 * g + 1], None)
                new.extend((carry[2 * g] + out, run))
            return tuple(new)

        carry = lax.fori_loop(0, i, body, tuple(carry))
        for g in range(len(group)):
            o_ref[0, :, lanes[g]] = carry[2 * g]


def _sb_prompt(q, k, v, bias):
    b, t, width = q.shape
    n_heads = width // HEAD_DIM
    tq = _divisor(t, SB_BLOCK)
    return pl.pallas_call(
        functools.partial(_sb_prompt_kernel, tq=tq, n_heads=n_heads),
        grid=(b, t // tq),
        in_specs=[
            pl.BlockSpec(memory_space=pltpu.SMEM),
            pl.BlockSpec((1, tq, width), lambda bi, i: (bi, i, 0)),
            pl.BlockSpec((1, t, width), lambda bi, i: (bi, 0, 0)),
            pl.BlockSpec((1, t, width), lambda bi, i: (bi, 0, 0)),
        ],
        out_specs=pl.BlockSpec((1, tq, width), lambda bi, i: (bi, i, 0)),
        out_shape=jax.ShapeDtypeStruct((b, t, width), F32),
        compiler_params=_cparams(("parallel", "arbitrary")),
        name="sb_prompt",
    )(bias, q, k, v)


SB_PAGES_PER_STEP = 8


def _sb_sample_kernel(pt_ref, bias_ref, q_ref, kn_ref, vn_ref, *refs, pps, n_heads, t_new):
    del pt_ref
    k_refs = refs[:pps]
    v_refs = refs[pps:2 * pps]
    o_ref, acc_ref, run_ref = refs[2 * pps:]
    step = pl.program_id(1)
    width = n_heads * HEAD_DIM
    cols = n_heads * t_new
    scale = HEAD_DIM ** -0.5
    q = q_ref[0] * scale
    q_exp = jnp.concatenate([q] * n_heads, axis=0)
    head_of_row = _iota((cols, width), 0) // t_new
    head_of_lane = _iota((cols, width), 1) // HEAD_DIM
    q_exp = jnp.where(head_of_row == head_of_lane, q_exp, 0.0).astype(BF16)
    bias_row = jnp.zeros((1, cols), F32)
    col_head = _iota((1, cols), 1) // t_new
    for h in range(n_heads):
        bias_row = jnp.where(col_head == h, bias_ref[h], bias_row)

    def later_matrix(nk):
        return jnp.where(_iota((nk, nk), 1) > _iota((nk, nk), 0), 1.0, 0.0).astype(BF16)

    @pl.when(step == 0)
    def _():
        readable = _iota((t_new, cols), 0) < _iota((t_new, cols), 1) % t_new
        z = lax.dot_general(kn_ref[0].astype(BF16), q_exp, NT, preferred_element_type=F32) + bias_row
        log_beta, log_skip = _sb_logs(z)
        log_skip = jnp.where(readable, log_skip, 0.0)
        later = _mm(later_matrix(t_new), log_skip, pb=2)
        w = jnp.where(readable, jnp.exp(log_beta + later), 0.0)
        acc_ref[...] = lax.dot_general(w.astype(BF16), vn_ref[0].astype(BF16), TN, preferred_element_type=F32)
        run_ref[...] = jnp.sum(log_skip, axis=0, keepdims=True)

    def page_rows(ref):
        return jnp.concatenate([ref[0, 0, pl.ds(h, PAGE_SIZE, stride=n_heads), :] for h in range(n_heads)],
                               axis=-1).astype(BF16)

    k_cat = jnp.concatenate([page_rows(r) for r in k_refs], axis=0)
    v_cat = jnp.concatenate([page_rows(r) for r in v_refs], axis=0)
    z = lax.dot_general(k_cat, q_exp, NT, preferred_element_type=F32) + bias_row
    log_beta, log_skip = _sb_logs(z)
    log_skip = log_skip.reshape(pps, PAGE_SIZE, cols)
    later = _mm(jnp.broadcast_to(later_matrix(PAGE_SIZE), (pps, PAGE_SIZE, PAGE_SIZE)), log_skip, BNN, pb=2)
    page_sum = jnp.sum(log_skip, axis=1, keepdims=True)
    run = run_ref[...]
    offsets = []
    for j in range(pps):
        offsets.append(run)
        run = run + page_sum[j]
    run_ref[...] = run
    w = jnp.exp(log_beta.reshape(pps, PAGE_SIZE, cols) + later + jnp.stack(offsets, axis=0))
    acc_ref[...] += lax.dot_general(w.reshape(pps * PAGE_SIZE, cols).astype(BF16), v_cat, TN,
                                    preferred_element_type=F32)

    @pl.when(step == pl.num_programs(1) - 1)
    def _():
        acc = acc_ref[...]
        lane_head = _iota((t_new, width), 1) // HEAD_DIM
        res = jnp.zeros((t_new, width), F32)
        for h in range(n_heads):
            res = res + jnp.where(lane_head == h, acc[h * t_new:(h + 1) * t_new, :], 0.0)
        o_ref[0] = res


def _sb_sample(q, k_new, v_new, cache_k, cache_v, page_table, bias, layer):
    b, t_new, width = q.shape
    n_pages = page_table.shape[1]
    n_heads = width // HEAD_DIM
    pps = SB_PAGES_PER_STEP if n_pages % SB_PAGES_PER_STEP == 0 else n_pages

    cache_k = cache_k.reshape(cache_k.shape[:2] + (PAGE_SIZE * n_heads, HEAD_DIM))
    cache_v = cache_v.reshape(cache_v.shape[:2] + (PAGE_SIZE * n_heads, HEAD_DIM))

    def page_spec(j):
        def index(bi, s, pt):
            return (layer, pt[bi * n_pages + (n_pages - 1 - (s * pps + j))], 0, 0)
        return pl.BlockSpec((1, 1, PAGE_SIZE * n_heads, HEAD_DIM), index)

    new_spec = pl.BlockSpec((1, t_new, width), lambda bi, s, pt: (bi, 0, 0))
    grid_spec = pltpu.PrefetchScalarGridSpec(
        num_scalar_prefetch=1,
        grid=(b, n_pages // pps),
        in_specs=[pl.BlockSpec(memory_space=pltpu.SMEM), new_spec, new_spec, new_spec]
        + [page_spec(j) for j in range(pps)] + [page_spec(j) for j in range(pps)],
        out_specs=new_spec,
        scratch_shapes=[pltpu.VMEM((n_heads * t_new, width), F32), pltpu.VMEM((1, n_heads * t_new), F32)],
    )
    return pl.pallas_call(
        functools.partial(_sb_sample_kernel, pps=pps, n_heads=n_heads, t_new=t_new),
        grid_spec=grid_spec,
        out_shape=jax.ShapeDtypeStruct((b, t_new, width), F32),
        compiler_params=_cparams(("parallel", "arbitrary")),
        name="sb_sample",
    )(page_table.reshape(-1), bias, q, k_new, v_new, *([cache_k] * pps), *([cache_v] * pps))


RWKV_SEQS = 8

BNN = (((2,), (1,)), ((0,), (0,)))
BNT = (((2,), (2,)), ((0,), (0,)))
BTN = (((1,), (1,)), ((0,), (0,)))


def _unit_lower_inverse(n_mat, size):
    row = _iota((size, size), 0)
    col = _iota((size, size), 1)
    eye = jnp.where(row == col, 1.0, 0.0)
    inv = eye + jnp.where(row // 2 == col // 2, n_mat, 0.0)
    s = 2
    while s < size:
        off = jnp.where((row // (2 * s) == col // (2 * s)) & (row // s != col // s), n_mat, 0.0)
        inv = inv + _mm(_mm(inv, off, BNN, pa=2, pb=2), inv, BNN, pa=2, pb=2)
        s *= 2
    return inv


def _rwkv_kernel(p_ref, shift_ref, s0_ref, mu_ref, w0_ref, wup_ref, a0_ref, aup_ref, gup_ref, kk_ref, ka_ref,
                 rk_ref, lnw_ref, lnb_ref, o_ref, s_ref, state, prev, *, chunk, n_heads, passes, bb):
    c = pl.program_id(1)
    width = n_heads * HEAD_DIM
    groups = bb * n_heads
    mm = functools.partial(_mm, pa=passes, pb=passes)

    @pl.when(c == 0)
    def _():
        state[...] = s0_ref[...].reshape(groups, HEAD_DIM, HEAD_DIM)
        prev[...] = shift_ref[...]

    p = p_ref[...]
    cols = p.shape[-1]
    rolled = pltpu.roll(p, 1, axis=1)
    prev_rows = jnp.where(_iota(p.shape, 1) == 0, prev[...], rolled)
    prev[...] = p[:, chunk - 1:chunk, :]
    xs = (p + (prev_rows - p) * mu_ref[...]).reshape(bb * chunk, cols)
    r = xs[:, 0:width]
    k = xs[:, width:2 * width]
    v = xs[:, 2 * width:3 * width]
    o1 = 3 * width
    wd = xs[:, o1:o1 + LANE]
    ad = xs[:, o1 + LANE:o1 + 2 * LANE]
    gd = xs[:, o1 + 2 * LANE:o1 + 2 * LANE + 2 * LANE]
    w_log = -_softplus(-(w0_ref[...] + mm(jnp.tanh(wd), wup_ref[...]))) - 0.5
    log_decay = -jnp.exp(w_log)
    a = jax.nn.sigmoid(a0_ref[...] + mm(ad, aup_ref[...]))
    g = mm(jax.nn.sigmoid(gd), gup_ref[...])
    kk_all = k * kk_ref[...]
    k2 = k * (1.0 + (a - 1.0) * ka_ref[...])

    def heads(x2):
        x3 = x2.reshape(bb, chunk, width)
        x4 = jnp.stack([x3[:, :, h * HEAD_DIM:(h + 1) * HEAD_DIM] for h in range(n_heads)], axis=1)
        return x4.reshape(groups, chunk, HEAD_DIM)

    def head_param(ref):
        row = ref[...]
        per_head = jnp.stack([row[:, h * HEAD_DIM:(h + 1) * HEAD_DIM] for h in range(n_heads)], axis=0)
        return jnp.concatenate([per_head] * bb, axis=0)

    row = _iota((chunk, chunk), 0)
    col = _iota((chunk, chunk), 1)
    strict = col < row
    lower = col <= row
    incl = jnp.broadcast_to(jnp.where(lower, 1.0, 0.0).astype(BF16), (groups, chunk, chunk))
    ld = heads(log_decay)
    bh = _mm(incl, ld, BNN, pb=3)
    kkh = heads(kk_all)
    kkh = kkh / jnp.maximum(jnp.sqrt(jnp.sum(kkh * kkh, axis=-1, keepdims=True)), 1e-12)
    rh, kh, vh, ah = heads(r), heads(k2), heads(v), heads(a)
    eb = jnp.exp(bh)
    einv = jnp.exp(-bh)
    al = -kkh * jnp.exp(bh - ld)
    be = kkh * ah * einv
    kb = kh * einv
    rb = rh * eb
    s0 = state[...]
    n_mat = jnp.where(strict, mm(al, be, BNT), 0.0)
    a_k = jnp.where(strict, mm(al, kb, BNT), 0.0)
    t_inv = _unit_lower_inverse(n_mat, chunk)
    u = _mm(t_inv, mm(al, s0, BNT) + mm(a_k, vh, BNN), BNN, pa=2, pb=2)
    y = (mm(rb, s0, BNT) + mm(jnp.where(lower, mm(rb, be, BNT), 0.0), u, BNN)
         + mm(jnp.where(lower, mm(rb, kb, BNT), 0.0), vh, BNN))
    state[...] = (s0 + mm(u, be, BTN) + mm(vh, kb, BTN)) * eb[:, chunk - 1:chunk, :]
    mean = jnp.mean(y, axis=-1, keepdims=True)
    var = jnp.mean(jnp.square(y - mean), axis=-1, keepdims=True)
    yn = (y - mean) * lax.rsqrt(var + RW_GN_EPS) * head_param(lnw_ref) + head_param(lnb_ref)
    bonus = jnp.sum(rh * kh * head_param(rk_ref), axis=-1, keepdims=True) * vh
    out = ((yn + bonus) * heads(g)).reshape(bb, n_heads, chunk, HEAD_DIM)
    for h in range(n_heads):
        o_ref[:, :, h * HEAD_DIM:(h + 1) * HEAD_DIM] = out[:, h]

    s_ref[...] = state[...].reshape(s_ref.shape)


def _rwkv(p_rw, shift, s0, prm, passes):
    b, t, cols = p_rw.shape
    n_heads = s0.shape[1]
    width = n_heads * HEAD_DIM
    chunk = _divisor(t, 64)
    bb = _divisor(b, RWKV_SEQS, 1)
    vec = lambda n: pl.BlockSpec((1, n), lambda bi, ci: (0, 0))
    mat = lambda m, n: pl.BlockSpec((m, n), lambda bi, ci: (0, 0))
    return pl.pallas_call(
        functools.partial(_rwkv_kernel, chunk=chunk, n_heads=n_heads, passes=passes, bb=bb),
        grid=(b // bb, t // chunk),
        in_specs=[
            pl.BlockSpec((bb, chunk, cols), lambda bi, ci: (bi, ci, 0)),
            pl.BlockSpec((bb, 1, cols), lambda bi, ci: (bi, 0, 0)),
            pl.BlockSpec((bb, n_heads, HEAD_DIM, HEAD_DIM), lambda bi, ci: (bi, 0, 0, 0)),
            vec(cols), vec(width), mat(LANE, width), vec(width), mat(LANE, width), mat(2 * LANE, width),
            vec(width), vec(width), vec(width), vec(width), vec(width),
        ],
        out_specs=[
            pl.BlockSpec((bb, chunk, width), lambda bi, ci: (bi, ci, 0)),
            pl.BlockSpec((bb, n_heads, HEAD_DIM, HEAD_DIM), lambda bi, ci: (bi, 0, 0, 0)),
        ],
        out_shape=[
            jax.ShapeDtypeStruct((b, t, width), F32),
            jax.ShapeDtypeStruct((b, n_heads, HEAD_DIM, HEAD_DIM), F32),
        ],
        scratch_shapes=[pltpu.VMEM((bb * n_heads, HEAD_DIM, HEAD_DIM), F32), pltpu.VMEM((bb, 1, cols), F32)],
        compiler_params=_cparams(("parallel", "arbitrary")),
        name="rwkv7",
    )(p_rw, shift, s0, prm["mu"], prm["w0"], prm["w_up"], prm["a0"], prm["a_up"], prm["g_up"],
      prm["k_k"], prm["k_a"], prm["r_k"], prm["ln_w"], prm["ln_b"])


HGRN_SEQS = 8


def _hgrn_kernel(p_ref, s0_ref, lbl_ref, ng_ref, o_ref, s_ref, st, *, chunk, n_heads, layer, passes, bb):
    c = pl.program_id(1)
    width = n_heads * HEAD_DIM
    mm = functools.partial(_mm, pa=passes, pb=passes)
    head_r = _iota((width, width), 0) // HEAD_DIM
    head_c = _iota((width, width), 1) // HEAD_DIM
    same_head = head_r == head_c
    ones_bd = jnp.where(same_head, 1.0, 0.0).astype(BF16)

    @pl.when(c == 0)
    def _():
        st[...] = jnp.zeros_like(st)
        for s in range(bb):
            for h in range(n_heads):
                lanes = slice(h * HEAD_DIM, (h + 1) * HEAD_DIM)
                st[s, lanes, lanes] = s0_ref[s, h].T

    logits = lbl_ref[...]
    e = jnp.exp(logits - jnp.max(logits, axis=0, keepdims=True))
    sm = e / jnp.sum(e, axis=0, keepdims=True)
    lb = jnp.zeros((1, width), F32)
    for j in range(1, layer + 1):
        lb = lb + sm[j:j + 1, :]

    p = p_ref[...]
    fp = p[:, :, 0:width]
    iv = p[:, :, width:2 * width]
    q = p[:, :, 2 * width:3 * width]
    go = p[:, :, 3 * width:4 * width]
    x1 = jnp.log(lb)
    x2 = jnp.log1p(-lb) + _log_sigmoid(fp)
    log_f = jnp.maximum(x1, x2) + jnp.log1p(jnp.exp(-jnp.abs(x1 - x2)))
    key = (1.0 - lb) * jax.nn.sigmoid(-fp)

    row = _iota((chunk, chunk), 0)
    col = _iota((chunk, chunk), 1)
    incl = jnp.broadcast_to(jnp.where(col <= row, 1.0, 0.0).astype(BF16), (bb, chunk, chunk))
    b = _mm(incl, log_f, BNN, pb=3)
    b_end = b[:, chunk - 1:chunk, :]

    st0 = st[...]
    inter = mm(q * jnp.exp(b), st0, BNT)
    diff = b[:, None, :, :] - b[:, :, None, :]
    shape4 = (bb, chunk, chunk, width)
    pair = jnp.where(_iota(shape4, 1) <= _iota(shape4, 2),
                     jnp.exp(jnp.minimum(diff, 0.0)) * q[:, None, :, :] * key[:, :, None, :], 0.0)
    att = _mm(pair.reshape(bb * chunk * chunk, width), ones_bd, pa=2).reshape(shape4)
    intra = jnp.sum(att * iv[:, :, None, :], axis=1)
    st[...] = st0 * jnp.exp(b_end) + jnp.where(same_head, mm(iv, key * jnp.exp(b_end - b), BTN), 0.0)

    o = inter + intra
    ms = _mm((o * o).reshape(bb * chunk, width), ones_bd, pa=2).reshape(o.shape) * (1.0 / HEAD_DIM)
    o_ref[...] = o * lax.rsqrt(ms + NORM_EPS) * ng_ref[...] * jax.nn.sigmoid(go)

    @pl.when(c == pl.num_programs(1) - 1)
    def _():
        for s in range(bb):
            for h in range(n_heads):
                lanes = slice(h * HEAD_DIM, (h + 1) * HEAD_DIM)
                s_ref[s, h] = st[s, lanes, lanes].T


def _hgrn(p_hg, s0, lb_logits, norm_g_tiled, layer, passes):
    b, t, cols = p_hg.shape
    n_heads = s0.shape[1]
    width = n_heads * HEAD_DIM
    chunk = _divisor(math.gcd(t, 64), 16)
    bb = _divisor(b, HGRN_SEQS, 1)
    depth = lb_logits.shape[0]
    return pl.pallas_call(
        functools.partial(_hgrn_kernel, chunk=chunk, n_heads=n_heads, layer=layer, passes=passes, bb=bb),
        grid=(b // bb, t // chunk),
        in_specs=[
            pl.BlockSpec((bb, chunk, cols), lambda bi, ci: (bi, ci, 0)),
            pl.BlockSpec((bb, n_heads, HEAD_DIM, HEAD_DIM), lambda bi, ci: (bi, 0, 0, 0)),
            pl.BlockSpec((depth, width), lambda bi, ci: (0, 0)),
            pl.BlockSpec((1, width), lambda bi, ci: (0, 0)),
        ],
        out_specs=[
            pl.BlockSpec((bb, chunk, width), lambda bi, ci: (bi, ci, 0)),
            pl.BlockSpec((bb, n_heads, HEAD_DIM, HEAD_DIM), lambda bi, ci: (bi, 0, 0, 0)),
        ],
        out_shape=[
            jax.ShapeDtypeStruct((b, t, width), F32),
            jax.ShapeDtypeStruct((b, n_heads, HEAD_DIM, HEAD_DIM), F32),
        ],
        scratch_shapes=[pltpu.VMEM((bb, width, width), F32)],
        compiler_params=_cparams(("parallel", "arbitrary")),
        name="hgrn2",
    )(p_hg, s0, lb_logits, norm_g_tiled)


def _merge_kernel(x_ref, gt_ref, sh_ref, sc_ref, osb_ref, orw_ref, ohg_ref, pg_ref, wsb_ref, wrw_ref, whg_ref,
                  wout_ref, ng_ref, rw_ref, rb_ref, xo_ref, h_ref, idx_ref, prob_ref, *, rows, d, passes):
    mm = functools.partial(_mm, pa=passes, pb=passes)

    def flat(ref):
        val = ref[...]
        return val.reshape(rows, val.shape[-1])

    gates = jax.nn.sigmoid(flat(pg_ref))
    merged = (gates[:, 0:d] * mm(flat(osb_ref), wsb_ref[...])
              + gates[:, d:2 * d] * mm(flat(orw_ref), wrw_ref[...])
              + gates[:, 2 * d:3 * d] * mm(flat(ohg_ref), whg_ref[...]))
    upd = mm(merged, wout_ref[...]).reshape(x_ref.shape)
    x = x_ref[...] + gt_ref[...] * upd
    xo_ref[...] = x
    ms = jnp.mean(x * x, axis=-1, keepdims=True)
    y = x * lax.rsqrt(ms + NORM_EPS) * ng_ref[...]
    h3 = y * (1.0 + sc_ref[...]) + sh_ref[...]
    h_ref[...] = h3.astype(h_ref.dtype)
    h = h3.reshape(rows, d)
    logits = _mm(h, rw_ref[...], pa=2, pb=2) + rb_ref[...]
    n_exp = logits.shape[-1]
    lane = _iota(logits.shape, 1)
    work = logits
    picks, vals = [], []
    for _ in range(TOP_K):
        m = jnp.max(work, axis=-1, keepdims=True)
        idx = jnp.min(jnp.where(work == m, lane, n_exp), axis=-1, keepdims=True)
        picks.append(idx)
        vals.append(m)
        work = jnp.where(lane == idx, -jnp.inf, work)
    exps = [jnp.exp(vv - vals[0]) for vv in vals]
    denom = exps[0]
    for ee in exps[1:]:
        denom = denom + ee
    out_lane = _iota((rows, LANE), 1)
    idx_out = jnp.zeros((rows, LANE), jnp.int32)
    prob_out = jnp.zeros((rows, LANE), F32)
    for kk, (idx, ee) in enumerate(zip(picks, exps)):
        idx_out = jnp.where(out_lane == kk, idx, idx_out)
        prob_out = jnp.where(out_lane == kk, ee / denom, prob_out)
    idx_ref[...] = idx_out.reshape(idx_ref.shape)
    prob_ref[...] = prob_out.reshape(prob_ref.shape)


def _merge(x3, gt, sh, sc, o_sb, o_rw, o_hg, p_gate, w_sb, w_rw, w_hg, w_out, norm_g, router_w, router_b,
           passes):
    b, t, d = x3.shape
    bb, tt = _row_tiles(b, t, 256)
    n_exp = router_w.shape[1]
    tile = lambda w: pl.BlockSpec((bb, tt, w), lambda i, j: (i, j, 0))
    per_seq = pl.BlockSpec((bb, 1, d), lambda i, j: (i, 0, 0))
    whole = lambda a: pl.BlockSpec(a.shape, lambda i, j: (0,) * a.ndim)
    ng = norm_g.reshape(1, 1, d)
    rb = router_b.reshape(1, n_exp)
    return pl.pallas_call(
        functools.partial(_merge_kernel, rows=bb * tt, d=d, passes=passes),
        grid=(b // bb, t // tt),
        in_specs=[tile(d), per_seq, per_seq, per_seq, tile(o_sb.shape[-1]), tile(o_rw.shape[-1]),
                  tile(o_hg.shape[-1]), tile(p_gate.shape[-1]), whole(w_sb), whole(w_rw), whole(w_hg),
                  whole(w_out), whole(ng), whole(router_w), whole(rb)],
        out_specs=[tile(d), tile(d), tile(LANE), tile(LANE)],
        out_shape=[jax.ShapeDtypeStruct((b, t, d), F32), jax.ShapeDtypeStruct((b, t, d), F32),
                   jax.ShapeDtypeStruct((b, t, LANE), jnp.int32), jax.ShapeDtypeStruct((b, t, LANE), F32)],
        compiler_params=_cparams(("parallel", "parallel")),
        name="merge",
    )(x3, gt, sh, sc, o_sb, o_rw, o_hg, p_gate, w_sb, w_rw, w_hg, w_out, ng, router_w, rb)


MOE_TILE = 256
RANK_TILE = 512
PAIR_TILE = 256


def _rank_kernel(e_ref, rank_ref, count_ref, carry, *, n_exp):
    i = pl.program_id(0)

    @pl.when(i == 0)
    def _():
        carry[...] = jnp.zeros_like(carry)

    e = e_ref[0]
    n = e.shape[1]
    hit = e == _iota((n_exp, n), 0)
    onehot = jnp.where(hit, 1.0, 0.0)
    earlier = jnp.where(_iota((n, n), 0) < _iota((n, n), 1), 1.0, 0.0).astype(BF16)
    before = lax.dot_general(onehot.astype(BF16), earlier, NN, preferred_element_type=F32) + carry[...]
    rank_ref[0] = jnp.sum(jnp.where(hit, before, 0.0), axis=0, keepdims=True).astype(jnp.int32)
    carry[...] = carry[...] + jnp.sum(onehot, axis=1, keepdims=True)
    count_ref[...] = carry[...].astype(jnp.int32)


def _rank(e_flat, n_exp):
    p = e_flat.shape[0]
    tile = _divisor(p, RANK_TILE, LANE)
    rank, count = pl.pallas_call(
        functools.partial(_rank_kernel, n_exp=n_exp),
        grid=(p // tile,),
        in_specs=[pl.BlockSpec((1, 1, tile), lambda i: (i, 0, 0))],
        out_specs=[pl.BlockSpec((1, 1, tile), lambda i: (i, 0, 0)), pl.BlockSpec((n_exp, 1), lambda i: (0, 0))],
        out_shape=[jax.ShapeDtypeStruct((p // tile, 1, tile), jnp.int32),
                   jax.ShapeDtypeStruct((n_exp, 1), jnp.int32)],
        scratch_shapes=[pltpu.VMEM((n_exp, 1), F32)],
        compiler_params=_cparams(("arbitrary",)),
        name="moe_rank",
    )(e_flat.reshape(p // tile, 1, tile))
    return rank.reshape(p), count.reshape(n_exp)


def _row_copies_wait(rows_ref, sem):
    pltpu.make_async_copy(rows_ref, rows_ref, sem).wait()


def _dispatch_kernel(slot_hbm, h_ref, init_hbm, out_hbm, slot_smem, sem_idx, sem_rows, *, tokens):
    del init_hbm
    i = pl.program_id(0)
    pairs = tokens * TOP_K
    idx_copy = pltpu.make_async_copy(slot_hbm.at[pl.ds(i * pairs, pairs)], slot_smem, sem_idx)
    idx_copy.start()
    idx_copy.wait()

    def issue(r, carry):
        for kk in range(TOP_K):
            slot = slot_smem[r * TOP_K + kk]
            pltpu.make_async_copy(h_ref.at[pl.ds(r, 1), :], out_hbm.at[pl.ds(slot, 1), :], sem_rows).start()
        return carry

    lax.fori_loop(0, tokens, issue, 0)
    for _ in range(TOP_K):
        _row_copies_wait(h_ref, sem_rows)


def _dispatch(h, slots, n_slots):
    n, d = h.shape
    tokens = _divisor(n, PAIR_TILE)
    return pl.pallas_call(
        functools.partial(_dispatch_kernel, tokens=tokens),
        grid=(n // tokens,),
        in_specs=[
            pl.BlockSpec(memory_space=pl.ANY),
            pl.BlockSpec((tokens, d), lambda i: (i, 0)),
            pl.BlockSpec(memory_space=pl.ANY),
        ],
        out_specs=pl.BlockSpec(memory_space=pl.ANY),
        out_shape=jax.ShapeDtypeStruct((n_slots, d), F32),
        scratch_shapes=[pltpu.SMEM((tokens * TOP_K,), jnp.int32), pltpu.SemaphoreType.DMA(()),
                        pltpu.SemaphoreType.DMA(())],
        input_output_aliases={2: 0},
        compiler_params=_cparams(("arbitrary",)),
        name="moe_dispatch",
    )(slots, h, jnp.zeros((n_slots, d), F32))


def _experts_kernel(te_ref, x_ref, w1_ref, b1_ref, w2_ref, b2_ref, o_ref, w1b, w2b, *, d_exp, n_used_idx):
    i = pl.program_id(0)
    expert = te_ref[i]
    fresh = jnp.logical_or(i == 0, expert != te_ref[jnp.maximum(i - 1, 0)])

    @pl.when(fresh)
    def _():
        rows = w1b.shape[0] // 8
        for c in range(8):
            w1b[c * rows:(c + 1) * rows, :] = w1_ref[0, c * rows:(c + 1) * rows, :].astype(BF16)
        rows = w2b.shape[0] // 8
        for c in range(8):
            w2b[c * rows:(c + 1) * rows, :] = w2_ref[0, c * rows:(c + 1) * rows, :].astype(BF16)

    used = i < te_ref[n_used_idx]

    @pl.when(used)
    def _():
        gu = lax.dot_general(x_ref[...].astype(BF16), w1b[...], NN, preferred_element_type=F32) + b1_ref[0]
        glu = jnp.minimum(gu[:, :d_exp], SWIGLU_LIMIT)
        lin = jnp.clip(gu[:, d_exp:], -SWIGLU_LIMIT, SWIGLU_LIMIT)
        act = glu * jax.nn.sigmoid(SWIGLU_ALPHA * glu) * (lin + 1.0)
        o_ref[...] = lax.dot_general(act.astype(BF16), w2b[...], NN, preferred_element_type=F32) + b2_ref[0]

    @pl.when(jnp.logical_not(used))
    def _():
        o_ref[...] = jnp.zeros_like(o_ref)


def _experts(x_slots, tile_expert, w1, b1, w2, b2, first_expert):
    s, d = x_slots.shape
    n_exp, _, two_de = w1.shape
    d_exp = two_de // 2
    n_tiles = s // MOE_TILE
    grid_spec = pltpu.PrefetchScalarGridSpec(
        num_scalar_prefetch=1,
        grid=(n_tiles,),
        in_specs=[
            pl.BlockSpec((MOE_TILE, d), lambda i, te: (i, 0)),
            pl.BlockSpec((1, d, two_de), lambda i, te: (first_expert + te[i], 0, 0)),
            pl.BlockSpec((1, 1, two_de), lambda i, te: (first_expert + te[i], 0, 0)),
            pl.BlockSpec((1, d_exp, d), lambda i, te: (first_expert + te[i], 0, 0)),
            pl.BlockSpec((1, 1, d), lambda i, te: (first_expert + te[i], 0, 0)),
        ],
        out_specs=pl.BlockSpec((MOE_TILE, d), lambda i, te: (i, 0)),
        scratch_shapes=[pltpu.VMEM((d, two_de), BF16), pltpu.VMEM((d_exp, d), BF16)],
    )
    return pl.pallas_call(
        functools.partial(_experts_kernel, d_exp=d_exp, n_used_idx=n_tiles),
        grid_spec=grid_spec,
        out_shape=jax.ShapeDtypeStruct((s, d), F32),
        compiler_params=_cparams(("arbitrary",)),
        name="moe_experts",
    )(tile_expert, x_slots, w1, b1.reshape(n_exp, 1, two_de), w2, b2.reshape(n_exp, 1, d))


def _combine_kernel(slot_hbm, y_hbm, prob_ref, o_ref, slot_smem, sem_idx, sem_rows, rows_buf, *, tokens, base):
    i = pl.program_id(0)
    pairs = tokens * TOP_K
    idx_copy = pltpu.make_async_copy(slot_hbm.at[pl.ds(base + i * pairs, pairs)], slot_smem, sem_idx)
    idx_copy.start()
    idx_copy.wait()

    def issue(r, carry):
        for kk in range(TOP_K):
            slot = slot_smem[r * TOP_K + kk]
            pltpu.make_async_copy(y_hbm.at[pl.ds(slot, 1), :], rows_buf.at[kk, pl.ds(r, 1), :], sem_rows).start()
        return carry

    lax.fori_loop(0, tokens, issue, 0)
    _row_copies_wait(rows_buf, sem_rows)
    prob = prob_ref[...]
    acc = prob[:, 0:1] * rows_buf[0]
    for kk in range(1, TOP_K):
        acc = acc + prob[:, kk:kk + 1] * rows_buf[kk]
    o_ref[...] = acc


def _combine(y_slots, slots, prob, first_token):
    n = prob.shape[0]
    d = y_slots.shape[1]
    tokens = _divisor(n, PAIR_TILE)
    return pl.pallas_call(
        functools.partial(_combine_kernel, tokens=tokens, base=first_token * TOP_K),
        grid=(n // tokens,),
        in_specs=[
            pl.BlockSpec(memory_space=pl.ANY),
            pl.BlockSpec(memory_space=pl.ANY),
            pl.BlockSpec((tokens, LANE), lambda i: (i, 0)),
        ],
        out_specs=pl.BlockSpec((tokens, d), lambda i: (i, 0)),
        out_shape=jax.ShapeDtypeStruct((n, d), F32),
        scratch_shapes=[pltpu.SMEM((tokens * TOP_K,), jnp.int32), pltpu.SemaphoreType.DMA(()),
                        pltpu.SemaphoreType.DMA(()), pltpu.VMEM((TOP_K, tokens, d), F32)],
        compiler_params=_cparams(("arbitrary",)),
        name="moe_combine",
    )(slots, y_slots, prob)


def _moe_plan(idx, n_exp):
    n = idx.shape[0]
    pairs = n * TOP_K
    e_flat = idx.reshape(pairs)
    rank, count = _rank(e_flat, n_exp)
    padded = ((count + MOE_TILE - 1) // MOE_TILE) * MOE_TILE
    group_end = jnp.cumsum(padded)
    group_start = group_end - padded
    slots = (jnp.take(group_start, e_flat) + rank).astype(jnp.int32)
    n_slots = ((pairs + MOE_TILE - 1) // MOE_TILE) * MOE_TILE + n_exp * MOE_TILE
    tile_start = jnp.arange(n_slots // MOE_TILE, dtype=jnp.int32) * MOE_TILE
    tile_expert = jnp.minimum(jnp.searchsorted(group_end, tile_start, side="right"), n_exp - 1).astype(jnp.int32)
    tiles_used = (group_end[-1] // MOE_TILE).astype(jnp.int32)
    return slots, jnp.concatenate([tile_expert, tiles_used[None]]), n_slots


def _pad_cols(a, n):
    return jnp.pad(a, [(0, 0)] * (a.ndim - 1) + [(0, n - a.shape[-1])])


def _pad_rows(a, n):
    return jnp.pad(a, [(0, n - a.shape[0]), (0, 0)])


def _rw_layout(a, rw_width):
    o1 = 3 * rw_width
    o2 = o1 + RW_DECAY_LORA
    o3 = o2 + RW_AAA_LORA
    return jnp.concatenate([
        a[..., :o1], _pad_cols(a[..., o1:o2], LANE), _pad_cols(a[..., o2:o3], LANE),
        _pad_cols(a[..., o3:], 2 * LANE)], axis=-1)


def _rw_unlayout(a, rw_width):
    o1 = 3 * rw_width
    return jnp.concatenate([
        a[..., :o1], a[..., o1:o1 + RW_DECAY_LORA], a[..., o1 + LANE:o1 + LANE + RW_AAA_LORA],
        a[..., o1 + 2 * LANE:o1 + 2 * LANE + RW_GATE_LORA]], axis=-1)


def _mixer_half(grp, l, w, mod_l):
    x = grp["x"]
    b, t, d = x.shape
    sb_w, rw_w = w["sb_w"], w["rw_w"]
    m = mod_l.reshape(b, 1, 6 * d)
    sh_a, sc_a, gt_a, sh_f, sc_f, gt_f = [m[:, :, i * d:(i + 1) * d] for i in range(6)]
    passes = w["passes"]
    delta = grp["delta"]
    x_new, h = _norm(x, delta, grp["gt_prev"], sh_a, sc_a, w["norm_attn_g"], emit_x=delta is not None,
                     h_dtype=BF16 if passes == 1 else F32)
    if x_new is not None:
        x = x_new
    hf = h.reshape(b * t, d)
    q, k, v = [a.reshape(b, t, sb_w) for a in _proj(hf, w["w_sb"], (sb_w, sb_w, sb_w), passes)]
    (p_rw,) = _proj(hf, w["w_rw"], (w["w_rw"].shape[1],), passes)
    (p_hg,) = _proj(hf, w["w_hg"], (w["w_hg"].shape[1],), passes)
    (p_gate,) = _proj(hf, w["w_gate"], (w["w_gate"].shape[1],), passes)
    p_rw = p_rw.reshape(b, t, -1)
    p_hg = p_hg.reshape(b, t, -1)
    p_gate = p_gate.reshape(b, t, -1)
    o_sb = grp["attn"](l, q, k, v, w["sb_bias"])
    o_rw, s_rw = _rwkv(p_rw, _rw_layout(grp["rw_shift"][l], rw_w).reshape(b, 1, -1), grp["rw_state"][l], w["rw"],
                       passes)
    o_hg, s_hg = _hgrn(p_hg, grp["hg_state"][l], w["hg_lb_logits"], w["hg_norm_g"], l, passes)
    x, h2, idx, prob = _merge(x, gt_a, sh_f, sc_f, o_sb, o_rw, o_hg, p_gate, w["w_branch_sb"], w["w_branch_rw"],
                              w["w_branch_hg"], w["w_out"], w["norm_ffn_g"], w["router_w"], w["router_b"], passes)
    grp["x"] = x
    grp["gt_prev"] = gt_f
    grp["ks"].append(k.reshape(b, t, sb_w // HEAD_DIM, HEAD_DIM))
    grp["vs"].append(v.reshape(b, t, sb_w // HEAD_DIM, HEAD_DIM))
    grp["rws"].append(s_rw)
    grp["shs"].append(_rw_unlayout(p_rw[:, -1], rw_w))
    grp["hgs"].append(s_hg)
    return h2.reshape(b * t, d), idx.reshape(b * t, LANE)[:, :TOP_K], prob.reshape(b * t, LANE)


def _run_layers(groups, mods, lw, moe, depth):
    n_exp = moe["w1"].shape[0] // depth
    for l in range(depth):
        halves = [_mixer_half(grp, l, lw[l], mod[l]) for grp, mod in zip(groups, mods)]
        h_all = jnp.concatenate([hh[0] for hh in halves], axis=0)
        idx_all = jnp.concatenate([hh[1] for hh in halves], axis=0)
        slots, tile_expert, n_slots = _moe_plan(idx_all, n_exp)
        x_slots = _dispatch(h_all, slots, n_slots)
        y_slots = _experts(x_slots, tile_expert, moe["w1"], moe["b1"], moe["w2"], moe["b2"], l * n_exp)
        first = 0
        for grp, hh in zip(groups, halves):
            b, t, d = grp["x"].shape
            grp["delta"] = _combine(y_slots, slots, hh[2], first).reshape(b, t, d)
            first += b * t
    outs = []
    for grp in groups:
        b, t, d = grp["x"].shape
        zeros = jnp.zeros((b, 1, d), F32)
        _, y = _norm(grp["x"], grp["delta"], grp["gt_prev"], zeros, zeros, lw[0]["final_norm_g"], emit_x=False,
                     h_dtype=F32)
        outs.append((y, jnp.stack(grp["ks"]), jnp.stack(grp["vs"]), jnp.stack(grp["rws"]), jnp.stack(grp["shs"]),
                     jnp.stack(grp["hgs"])))
    return outs


def kernel(x_prompt, x_sample, cache_sb_k, cache_sb_v, state_rwkv, state_rwkv_shift, state_hgrn, page_table,
           c_prompt, c_sample, w_ada, b_ada, norm_attn_g, norm_ffn_g, w_in, sb_bias, rw_mu, rw_w0, rw_w_up, rw_a0,
           rw_a_up, rw_g_up, rw_k_k, rw_k_a, rw_r_k, rw_ln_w, rw_ln_b, hg_lb_logits, hg_norm_g, w_branch_sb,
           w_branch_rw, w_branch_hg, w_out, router_w, router_b, moe_w1, moe_b1, moe_w2, moe_b2, final_norm_g):
    depth, d, _ = w_in.shape
    bp = x_prompt.shape[0]
    db = x_sample.shape[0]
    sb_w = w_branch_sb.shape[1]
    rw_w = w_branch_rw.shape[1]
    hg_w = w_branch_hg.shape[1]
    rw_heads = rw_w // HEAD_DIM
    hg_heads = hg_w // HEAD_DIM
    rw_cols = 3 * rw_w + RW_DECAY_LORA + RW_AAA_LORA + RW_GATE_LORA
    o_rw = 3 * sb_w
    o_hg = o_rw + rw_cols
    o_gate = o_hg + 4 * hg_w

    mod = _ada(jnp.concatenate([c_prompt, c_sample], axis=0), w_ada, b_ada)

    lw = []
    for l in range(depth):
        wl = w_in[l]
        passes = 2 if l + 1 < depth else 1
        wdt = BF16 if passes == 1 else F32
        lw.append(dict(
            sb_w=sb_w, rw_w=rw_w, passes=passes,
            w_sb=wl[:, :o_rw].astype(wdt),
            w_rw=_rw_layout(wl[:, o_rw:o_hg], rw_w).astype(wdt),
            w_hg=wl[:, o_hg:o_gate].astype(wdt),
            w_gate=wl[:, o_gate:].astype(wdt),
            norm_attn_g=norm_attn_g[l], norm_ffn_g=norm_ffn_g[l], final_norm_g=final_norm_g,
            sb_bias=sb_bias[l],
            rw=dict(
                mu=_rw_layout(rw_mu[l][None, :], rw_w), w0=rw_w0[l][None, :],
                w_up=_pad_rows(rw_w_up[l], LANE), a0=rw_a0[l][None, :], a_up=_pad_rows(rw_a_up[l], LANE),
                g_up=_pad_rows(rw_g_up[l], 2 * LANE), k_k=rw_k_k[l][None, :], k_a=rw_k_a[l][None, :],
                r_k=rw_r_k[l].reshape(1, rw_w), ln_w=rw_ln_w[l][None, :], ln_b=rw_ln_b[l][None, :]),
            hg_lb_logits=hg_lb_logits,
            hg_norm_g=jnp.tile(hg_norm_g[l], hg_heads)[None, :],
            w_branch_sb=w_branch_sb[l].astype(wdt), w_branch_rw=w_branch_rw[l].astype(wdt),
            w_branch_hg=w_branch_hg[l].astype(wdt), w_out=w_out[l].astype(wdt),
            router_w=router_w[l], router_b=router_b[l],
        ))
    n_exp = moe_w1.shape[1]
    moe = dict(w1=moe_w1.reshape((depth * n_exp,) + moe_w1.shape[2:]), b1=moe_b1.reshape(depth * n_exp, -1),
               w2=moe_w2.reshape((depth * n_exp,) + moe_w2.shape[2:]), b2=moe_b2.reshape(depth * n_exp, -1))

    def prompt_attn(l, q, k, v, bias):
        del l
        return _sb_prompt(q, k, v, bias)

    def sample_attn(l, q, k, v, bias):
        return _sb_sample(q, k, v, cache_sb_k, cache_sb_v, page_table, bias, l)

    zeros_rw = jnp.zeros((depth, bp, rw_heads, HEAD_DIM, HEAD_DIM), F32)
    zeros_sh = jnp.zeros((depth, bp, rw_cols), F32)
    zeros_hg = jnp.zeros((depth, bp, hg_heads, HEAD_DIM, HEAD_DIM), F32)

    def group(x, attn, rw_state, rw_shift, hg_state):
        return dict(x=x, attn=attn, rw_state=rw_state, rw_shift=rw_shift, hg_state=hg_state, delta=None,
                    gt_prev=None, ks=[], vs=[], rws=[], shs=[], hgs=[])

    groups = [group(x_prompt, prompt_attn, zeros_rw, zeros_sh, zeros_hg),
              group(x_sample, sample_attn, state_rwkv, state_rwkv_shift, state_hgrn)]
    out_p, out_s = _run_layers(groups, [mod[:, :bp], mod[:, bp:]], lw, moe, depth)
    return (out_p[0], out_s[0]) + tuple(out_p[1:]) + tuple(out_s[1:])
```

```python
import functools
import math

import jax
import jax.numpy as jnp
from jax import lax
from jax.experimental import pallas as pl
from jax.experimental.pallas import tpu as pltpu

F32 = jnp.float32
BF16 = jnp.bfloat16

HEAD_DIM = 64
PAGE_SIZE = 128
TOP_K = 4
SWIGLU_LIMIT = 7.0
SWIGLU_ALPHA = 1.702
NORM_EPS = 1e-5
RW_GN_EPS = HEAD_DIM * 1e-5
RW_DECAY_LORA = 64
RW_AAA_LORA = 64
RW_GATE_LORA = 160
LANE = 128
VMEM_LIMIT = 48 * 1024 * 1024

NN = (((1,), (0,)), ((), ()))
NT = (((1,), (1,)), ((), ()))
TN = (((0,), (0,)), ((), ()))


def _cparams(sem):
    return pltpu.CompilerParams(dimension_semantics=sem, vmem_limit_bytes=VMEM_LIMIT)


def _split(x, n):
    parts = []
    r = x
    for i in range(n):
        p = r.astype(BF16)
        parts.append(p)
        if i + 1 < n:
            r = r - p.astype(F32)
    return parts


MXU_DEPTH = 256


def _mm(a, b, dims=NN, pa=1, pb=1):
    a_parts = _split(a, pa) if a.dtype != BF16 else [a]
    b_parts = _split(b, pb) if b.dtype != BF16 else [b]
    order = max(len(a_parts), len(b_parts))
    terms = [(ap, bp) for i, ap in enumerate(a_parts) for j, bp in enumerate(b_parts) if i + j < order]
    (ca,), (cb,) = dims[0]
    if len(terms) > 1 and len(terms) * a.shape[ca] <= MXU_DEPTH:
        a_cat = jnp.concatenate([t[0] for t in terms], axis=ca)
        b_cat = jnp.concatenate([t[1] for t in terms], axis=cb)
        return lax.dot_general(a_cat, b_cat, dims, preferred_element_type=F32)
    out = None
    for ap, bp in terms:
        t = lax.dot_general(ap, bp, dims, preferred_element_type=F32)
        out = t if out is None else out + t
    return out


def _log_sigmoid(z):
    return jnp.minimum(z, 0.0) - jnp.log1p(jnp.exp(-jnp.abs(z)))


def _sb_logs(z):
    soft = jnp.log(1.0 + jnp.exp(-jnp.abs(z)))
    return jnp.minimum(z, 0.0) - soft, -jnp.maximum(z, 0.0) - soft


def _softplus(z):
    return jnp.maximum(z, 0.0) + jnp.log1p(jnp.exp(-jnp.abs(z)))


def _iota(shape, dim):
    return lax.broadcasted_iota(jnp.int32, shape, dim)


def _divisor(n, target, mult=8):
    if n <= target:
        return n
    for d in range(target, 0, -1):
        if n % d == 0 and d % mult == 0:
            return d
    return n


def _row_tiles(batch, seq, target):
    if seq >= target:
        return 1, _divisor(seq, target)
    bb = 1
    for d in range(1, batch + 1):
        if batch % d == 0 and d * seq <= target:
            bb = d
    return bb, seq


def _ada_kernel(c_ref, w_ref, b_ref, o_ref):
    c = c_ref[...]
    act = c * jax.nn.sigmoid(c)
    o_ref[0] = _mm(act, w_ref[0], pa=2, pb=2) + b_ref[0]


def _ada(c_all, w_ada, b_ada):
    depth, d, n6 = w_ada.shape
    rows = c_all.shape[0]
    tn = _divisor(n6, 1536, LANE)
    return pl.pallas_call(
        _ada_kernel,
        grid=(depth, n6 // tn),
        in_specs=[
            pl.BlockSpec((rows, d), lambda l, j: (0, 0)),
            pl.BlockSpec((1, d, tn), lambda l, j: (l, 0, j)),
            pl.BlockSpec((1, 1, tn), lambda l, j: (l, 0, j)),
        ],
        out_specs=pl.BlockSpec((1, rows, tn), lambda l, j: (l, 0, j)),
        out_shape=jax.ShapeDtypeStruct((depth, rows, n6), F32),
        compiler_params=_cparams(("parallel", "parallel")),
        name="ada",
    )(c_all, w_ada, b_ada.reshape(depth, 1, n6))


def _norm_kernel(*refs, has_delta, emit_x):
    it = iter(refs)
    x_ref = next(it)
    if has_delta:
        d_ref = next(it)
        gt_ref = next(it)
    sh_ref = next(it)
    sc_ref = next(it)
    g_ref = next(it)
    if emit_x:
        xo_ref = next(it)
    h_ref = next(it)
    x = x_ref[...]
    if has_delta:
        x = x + gt_ref[...] * d_ref[...]
    if emit_x:
        xo_ref[...] = x
    ms = jnp.mean(x * x, axis=-1, keepdims=True)
    y = x * lax.rsqrt(ms + NORM_EPS) * g_ref[...]
    h_ref[...] = (y * (1.0 + sc_ref[...]) + sh_ref[...]).astype(h_ref.dtype)


def _norm(x3, delta3, gt, sh, sc, g, *, emit_x, h_dtype):
    b, t, d = x3.shape
    bb, tt = _row_tiles(b, t, 512)
    tile = pl.BlockSpec((bb, tt, d), lambda i, j: (i, j, 0))
    per_seq = pl.BlockSpec((bb, 1, d), lambda i, j: (i, 0, 0))
    has_delta = delta3 is not None
    args, in_specs = [x3], [tile]
    if has_delta:
        args += [delta3, gt]
        in_specs += [tile, per_seq]
    args += [sh, sc, g.reshape(1, 1, d)]
    in_specs += [per_seq, per_seq, pl.BlockSpec((1, 1, d), lambda i, j: (0, 0, 0))]
    out_shape, out_specs = [], []
    if emit_x:
        out_shape.append(jax.ShapeDtypeStruct((b, t, d), F32))
        out_specs.append(tile)
    out_shape.append(jax.ShapeDtypeStruct((b, t, d), h_dtype))
    out_specs.append(tile)
    outs = pl.pallas_call(
        functools.partial(_norm_kernel, has_delta=has_delta, emit_x=emit_x),
        grid=(b // bb, t // tt),
        in_specs=in_specs,
        out_specs=out_specs,
        out_shape=out_shape,
        compiler_params=_cparams(("parallel", "parallel")),
        name="norm",
    )(*args)
    return outs if emit_x else (None, outs[0])


def _proj_kernel(h_ref, w_ref, *o_refs, widths, chunk, passes):
    h = h_ref[...]
    col = 0
    for o_ref, width in zip(o_refs, widths):
        for c0 in range(0, width, chunk):
            c1 = min(c0 + chunk, width)
            o_ref[:, c0:c1] = _mm(h, w_ref[:, col + c0:col + c1], pa=passes, pb=passes)
        col += width


def _proj(h, w, widths, passes):
    n, k = h.shape
    tm = _divisor(n, 512 if passes == 1 else 256)
    return pl.pallas_call(
        functools.partial(_proj_kernel, widths=tuple(widths), chunk=512, passes=passes),
        grid=(n // tm,),
        in_specs=[
            pl.BlockSpec((tm, k), lambda i: (i, 0)),
            pl.BlockSpec((k, w.shape[1]), lambda i: (0, 0)),
        ],
        out_specs=[pl.BlockSpec((tm, wd), lambda i: (i, 0)) for wd in widths],
        out_shape=[jax.ShapeDtypeStruct((n, wd), F32) for wd in widths],
        compiler_params=_cparams(("parallel",)),
        name="proj",
    )(h, w)


def _sb_weights(z, run, upper, readable):
    log_beta, log_skip = _sb_logs(z)
    if readable is not None:
        log_skip = jnp.where(readable, log_skip, 0.0)
    later = _mm(log_skip, upper, pa=2) + run
    w = jnp.exp(log_beta + later)
    if readable is not None:
        w = jnp.where(readable, w, 0.0)
    return w, run + jnp.sum(log_skip, axis=-1, keepdims=True)


SB_HEADS_PER_LOOP = 4
SB_BLOCK = 256


def _sb_prompt_kernel(bias_ref, q_ref, k_ref, v_ref, o_ref, *, tq, n_heads):
    i = pl.program_id(1)
    scale = HEAD_DIM ** -0.5
    row = _iota((tq, tq), 0)
    col = _iota((tq, tq), 1)
    upper = jnp.where(row > col, 1.0, 0.0).astype(BF16)
    diag_readable = col < row
    for h0 in range(0, n_heads, SB_HEADS_PER_LOOP):
        group = list(range(h0, min(h0 + SB_HEADS_PER_LOOP, n_heads)))
        lanes = [slice(h * HEAD_DIM, (h + 1) * HEAD_DIM) for h in group]
        qs = [(q_ref[0, :, ln] * scale).astype(BF16) for ln in lanes]
        biases = [bias_ref[h] for h in group]

        def block(j, g, run, readable):
            start = pl.multiple_of(j * tq, tq)
            kh = k_ref[0, pl.ds(start, tq), lanes[g]].astype(BF16)
            vh = v_ref[0, pl.ds(start, tq), lanes[g]].astype(BF16)
            z = lax.dot_general(qs[g], kh, NT, preferred_element_type=F32) + biases[g]
            w, run = _sb_weights(z, run, upper, readable)
            return lax.dot_general(w.astype(BF16), vh, NN, preferred_element_type=F32), run

        carry = []
        for g in range(len(group)):
            carry.extend(block(i, g, jnp.zeros((tq, 1), F32), diag_readable))

        def body(jj, carry):
            new = []
            for g in range(len(group)):
                out, run = block(i - 1 - jj, g, carry[2 * g + 1], None)
                new.extend((carry[2 * g] + out, run))
            return tuple(new)

        carry = lax.fori_loop(0, i, body, tuple(carry))
        for g in range(len(group)):
            o_ref[0, :, lanes[g]] = carry[2 * g]


def _sb_prompt(q, k, v, bias):
    b, t, width = q.shape
    n_heads = width // HEAD_DIM
    tq = _divisor(t, SB_BLOCK)
    return pl.pallas_call(
        functools.partial(_sb_prompt_kernel, tq=tq, n_heads=n_heads),
        grid=(b, t // tq),
        in_specs=[
            pl.BlockSpec(memory_space=pltpu.SMEM),
            pl.BlockSpec((1, tq, width), lambda bi, i: (bi, i, 0)),
            pl.BlockSpec((1, t, width), lambda bi, i: (bi, 0, 0)),
            pl.BlockSpec((1, t, width), lambda bi, i: (bi, 0, 0)),
        ],
        out_specs=pl.BlockSpec((1, tq, width), lambda bi, i: (bi, i, 0)),
        out_shape=jax.ShapeDtypeStruct((b, t, width), F32),
        compiler_params=_cparams(("parallel", "arbitrary")),
        name="sb_prompt",
    )(bias, q, k, v)


SB_PAGES_PER_STEP = 8


def _sb_sample_kernel(pt_ref, bias_ref, q_ref, kn_ref, vn_ref, *refs, pps, n_heads, t_new):
    del pt_ref
    k_refs = refs[:pps]
    v_refs = refs[pps:2 * pps]
    o_ref, acc_ref, run_ref = refs[2 * pps:]
    step = pl.program_id(1)
    width = n_heads * HEAD_DIM
    cols = n_heads * t_new
    scale = HEAD_DIM ** -0.5
    q = q_ref[0] * scale
    q_exp = jnp.concatenate([q] * n_heads, axis=0)
    head_of_row = _iota((cols, width), 0) // t_new
    head_of_lane = _iota((cols, width), 1) // HEAD_DIM
    q_exp = jnp.where(head_of_row == head_of_lane, q_exp, 0.0).astype(BF16)
    bias_row = jnp.zeros((1, cols), F32)
    col_head = _iota((1, cols), 1) // t_new
    for h in range(n_heads):
        bias_row = jnp.where(col_head == h, bias_ref[h], bias_row)

    def later_matrix(nk):
        return jnp.where(_iota((nk, nk), 1) > _iota((nk, nk), 0), 1.0, 0.0).astype(BF16)

    @pl.when(step == 0)
    def _():
        readable = _iota((t_new, cols), 0) < _iota((t_new, cols), 1) % t_new
        z = lax.dot_general(kn_ref[0].astype(BF16), q_exp, NT, preferred_element_type=F32) + bias_row
        log_beta, log_skip = _sb_logs(z)
        log_skip = jnp.where(readable, log_skip, 0.0)
        later = _mm(later_matrix(t_new), log_skip, pb=2)
        w = jnp.where(readable, jnp.exp(log_beta + later), 0.0)
        acc_ref[...] = lax.dot_general(w.astype(BF16), vn_ref[0].astype(BF16), TN, preferred_element_type=F32)
        run_ref[...] = jnp.sum(log_skip, axis=0, keepdims=True)

    def page_rows(ref):
        return pltpu.einshape("thd->t(hd)", ref[0, 0]).astype(BF16)

    k_cat = jnp.concatenate([page_rows(r) for r in k_refs], axis=0)
    v_cat = jnp.concatenate([page_rows(r) for r in v_refs], axis=0)
    z = lax.dot_general(k_cat, q_exp, NT, preferred_element_type=F32) + bias_row
    log_beta, log_skip = _sb_logs(z)
    log_skip = log_skip.reshape(pps, PAGE_SIZE, cols)
    later = _mm(jnp.broadcast_to(later_matrix(PAGE_SIZE), (pps, PAGE_SIZE, PAGE_SIZE)), log_skip, BNN, pb=2)
    page_sum = jnp.sum(log_skip, axis=1, keepdims=True)
    run = run_ref[...]
    offsets = []
    for j in range(pps):
        offsets.append(run)
        run = run + page_sum[j]
    run_ref[...] = run
    w = jnp.exp(log_beta.reshape(pps, PAGE_SIZE, cols) + later + jnp.stack(offsets, axis=0))
    acc_ref[...] += lax.dot_general(w.reshape(pps * PAGE_SIZE, cols).astype(BF16), v_cat, TN,
                                    preferred_element_type=F32)

    @pl.when(step == pl.num_programs(1) - 1)
    def _():
        acc = acc_ref[...]
        lane_head = _iota((t_new, width), 1) // HEAD_DIM
        res = jnp.zeros((t_new, width), F32)
        for h in range(n_heads):
            res = res + jnp.where(lane_head == h, acc[h * t_new:(h + 1) * t_new, :], 0.0)
        o_ref[0] = res


def _sb_sample(q, k_new, v_new, cache_k, cache_v, page_table, bias, layer):
    b, t_new, width = q.shape
    n_pages = page_table.shape[1]
    n_heads = width // HEAD_DIM
    pps = SB_PAGES_PER_STEP if n_pages % SB_PAGES_PER_STEP == 0 else n_pages

    def page_spec(j):
        def index(bi, s, pt):
            return (layer, pt[bi * n_pages + (n_pages - 1 - (s * pps + j))], 0, 0, 0)
        return pl.BlockSpec((1, 1, PAGE_SIZE, n_heads, HEAD_DIM), index)

    new_spec = pl.BlockSpec((1, t_new, width), lambda bi, s, pt: (bi, 0, 0))
    grid_spec = pltpu.PrefetchScalarGridSpec(
        num_scalar_prefetch=1,
        grid=(b, n_pages // pps),
        in_specs=[pl.BlockSpec(memory_space=pltpu.SMEM), new_spec, new_spec, new_spec]
        + [page_spec(j) for j in range(pps)] + [page_spec(j) for j in range(pps)],
        out_specs=new_spec,
        scratch_shapes=[pltpu.VMEM((n_heads * t_new, width), F32), pltpu.VMEM((1, n_heads * t_new), F32)],
    )
    return pl.pallas_call(
        functools.partial(_sb_sample_kernel, pps=pps, n_heads=n_heads, t_new=t_new),
        grid_spec=grid_spec,
        out_shape=jax.ShapeDtypeStruct((b, t_new, width), F32),
        compiler_params=_cparams(("parallel", "arbitrary")),
        name="sb_sample",
    )(page_table.reshape(-1), bias, q, k_new, v_new, *([cache_k] * pps), *([cache_v] * pps))


RWKV_SEQS = 8

BNN = (((2,), (1,)), ((0,), (0,)))
BNT = (((2,), (2,)), ((0,), (0,)))
BTN = (((1,), (1,)), ((0,), (0,)))


def _unit_lower_inverse(n_mat, size):
    row = _iota((size, size), 0)
    col = _iota((size, size), 1)
    eye = jnp.where(row == col, 1.0, 0.0)
    inv = eye + jnp.where(row // 2 == col // 2, n_mat, 0.0)
    s = 2
    while s < size:
        off = jnp.where((row // (2 * s) == col // (2 * s)) & (row // s != col // s), n_mat, 0.0)
        inv = inv + _mm(_mm(inv, off, BNN, pa=2, pb=2), inv, BNN, pa=2, pb=2)
        s *= 2
    return inv


def _rwkv_kernel(p_ref, shift_ref, s0_ref, mu_ref, w0_ref, wup_ref, a0_ref, aup_ref, gup_ref, kk_ref, ka_ref,
                 rk_ref, lnw_ref, lnb_ref, o_ref, s_ref, state, prev, *, chunk, n_heads, passes, bb):
    c = pl.program_id(1)
    width = n_heads * HEAD_DIM
    groups = bb * n_heads
    mm = functools.partial(_mm, pa=passes, pb=passes)

    @pl.when(c == 0)
    def _():
        state[...] = s0_ref[...].reshape(groups, HEAD_DIM, HEAD_DIM)
        prev[...] = shift_ref[...]

    p = p_ref[...]
    cols = p.shape[-1]
    rolled = pltpu.roll(p, 1, axis=1)
    prev_rows = jnp.where(_iota(p.shape, 1) == 0, prev[...], rolled)
    prev[...] = p[:, chunk - 1:chunk, :]
    xs = (p + (prev_rows - p) * mu_ref[...]).reshape(bb * chunk, cols)
    r = xs[:, 0:width]
    k = xs[:, width:2 * width]
    v = xs[:, 2 * width:3 * width]
    o1 = 3 * width
    wd = xs[:, o1:o1 + LANE]
    ad = xs[:, o1 + LANE:o1 + 2 * LANE]
    gd = xs[:, o1 + 2 * LANE:o1 + 2 * LANE + 2 * LANE]
    w_log = -_softplus(-(w0_ref[...] + mm(jnp.tanh(wd), wup_ref[...]))) - 0.5
    log_decay = -jnp.exp(w_log)
    a = jax.nn.sigmoid(a0_ref[...] + mm(ad, aup_ref[...]))
    g = mm(jax.nn.sigmoid(gd), gup_ref[...])
    kk_all = k * kk_ref[...]
    k2 = k * (1.0 + (a - 1.0) * ka_ref[...])

    def heads(x2):
        x3 = x2.reshape(bb, chunk, width)
        x4 = jnp.stack([x3[:, :, h * HEAD_DIM:(h + 1) * HEAD_DIM] for h in range(n_heads)], axis=1)
        return x4.reshape(groups, chunk, HEAD_DIM)

    def head_param(ref):
        row = ref[...]
        per_head = jnp.stack([row[:, h * HEAD_DIM:(h + 1) * HEAD_DIM] for h in range(n_heads)], axis=0)
        return jnp.concatenate([per_head] * bb, axis=0)

    row = _iota((chunk, chunk), 0)
    col = _iota((chunk, chunk), 1)
    strict = col < row
    lower = col <= row
    incl = jnp.broadcast_to(jnp.where(lower, 1.0, 0.0).astype(BF16), (groups, chunk, chunk))
    ld = heads(log_decay)
    bh = _mm(incl, ld, BNN, pb=3)
    kkh = heads(kk_all)
    kkh = kkh / jnp.maximum(jnp.sqrt(jnp.sum(kkh * kkh, axis=-1, keepdims=True)), 1e-12)
    rh, kh, vh, ah = heads(r), heads(k2), heads(v), heads(a)
    eb = jnp.exp(bh)
    einv = jnp.exp(-bh)
    al = -kkh * jnp.exp(bh - ld)
    be = kkh * ah * einv
    kb = kh * einv
    rb = rh * eb
    s0 = state[...]
    n_mat = jnp.where(strict, mm(al, be, BNT), 0.0)
    a_k = jnp.where(strict, mm(al, kb, BNT), 0.0)
    t_inv = _unit_lower_inverse(n_mat, chunk)
    u = _mm(t_inv, mm(al, s0, BNT) + mm(a_k, vh, BNN), BNN, pa=2, pb=2)
    y = (mm(rb, s0, BNT) + mm(jnp.where(lower, mm(rb, be, BNT), 0.0), u, BNN)
         + mm(jnp.where(lower, mm(rb, kb, BNT), 0.0), vh, BNN))
    state[...] = (s0 + mm(u, be, BTN) + mm(vh, kb, BTN)) * eb[:, chunk - 1:chunk, :]
    mean = jnp.mean(y, axis=-1, keepdims=True)
    var = jnp.mean(jnp.square(y - mean), axis=-1, keepdims=True)
    yn = (y - mean) * lax.rsqrt(var + RW_GN_EPS) * head_param(lnw_ref) + head_param(lnb_ref)
    bonus = jnp.sum(rh * kh * head_param(rk_ref), axis=-1, keepdims=True) * vh
    out = ((yn + bonus) * heads(g)).reshape(bb, n_heads, chunk, HEAD_DIM)
    for h in range(n_heads):
        o_ref[:, :, h * HEAD_DIM:(h + 1) * HEAD_DIM] = out[:, h]

    s_ref[...] = state[...].reshape(s_ref.shape)


def _rwkv(p_rw, shift, s0, prm, passes):
    b, t, cols = p_rw.shape
    n_heads = s0.shape[1]
    width = n_heads * HEAD_DIM
    chunk = _divisor(t, 64)
    bb = _divisor(b, RWKV_SEQS, 1)
    vec = lambda n: pl.BlockSpec((1, n), lambda bi, ci: (0, 0))
    mat = lambda m, n: pl.BlockSpec((m, n), lambda bi, ci: (0, 0))
    return pl.pallas_call(
        functools.partial(_rwkv_kernel, chunk=chunk, n_heads=n_heads, passes=passes, bb=bb),
        grid=(b // bb, t // chunk),
        in_specs=[
            pl.BlockSpec((bb, chunk, cols), lambda bi, ci: (bi, ci, 0)),
            pl.BlockSpec((bb, 1, cols), lambda bi, ci: (bi, 0, 0)),
            pl.BlockSpec((bb, n_heads, HEAD_DIM, HEAD_DIM), lambda bi, ci: (bi, 0, 0, 0)),
            vec(cols), vec(width), mat(LANE, width), vec(width), mat(LANE, width), mat(2 * LANE, width),
            vec(width), vec(width), vec(width), vec(width), vec(width),
        ],
        out_specs=[
            pl.BlockSpec((bb, chunk, width), lambda bi, ci: (bi, ci, 0)),
            pl.BlockSpec((bb, n_heads, HEAD_DIM, HEAD_DIM), lambda bi, ci: (bi, 0, 0, 0)),
        ],
        out_shape=[
            jax.ShapeDtypeStruct((b, t, width), F32),
            jax.ShapeDtypeStruct((b, n_heads, HEAD_DIM, HEAD_DIM), F32),
        ],
        scratch_shapes=[pltpu.VMEM((bb * n_heads, HEAD_DIM, HEAD_DIM), F32), pltpu.VMEM((bb, 1, cols), F32)],
        compiler_params=_cparams(("parallel", "arbitrary")),
        name="rwkv7",
    )(p_rw, shift, s0, prm["mu"], prm["w0"], prm["w_up"], prm["a0"], prm["a_up"], prm["g_up"],
      prm["k_k"], prm["k_a"], prm["r_k"], prm["ln_w"], prm["ln_b"])


HGRN_SEQS = 8


def _hgrn_kernel(p_ref, s0_ref, lbl_ref, ng_ref, o_ref, s_ref, st, *, chunk, n_heads, layer, passes, bb):
    c = pl.program_id(1)
    width = n_heads * HEAD_DIM
    mm = functools.partial(_mm, pa=passes, pb=passes)
    head_r = _iota((width, width), 0) // HEAD_DIM
    head_c = _iota((width, width), 1) // HEAD_DIM
    same_head = head_r == head_c
    ones_bd = jnp.where(same_head, 1.0, 0.0).astype(BF16)

    @pl.when(c == 0)
    def _():
        st[...] = jnp.zeros_like(st)
        for s in range(bb):
            for h in range(n_heads):
                lanes = slice(h * HEAD_DIM, (h + 1) * HEAD_DIM)
                st[s, lanes, lanes] = s0_ref[s, h].T

    logits = lbl_ref[...]
    e = jnp.exp(logits - jnp.max(logits, axis=0, keepdims=True))
    sm = e / jnp.sum(e, axis=0, keepdims=True)
    lb = jnp.zeros((1, width), F32)
    for j in range(1, layer + 1):
        lb = lb + sm[j:j + 1, :]

    p = p_ref[...]
    fp = p[:, :, 0:width]
    iv = p[:, :, width:2 * width]
    q = p[:, :, 2 * width:3 * width]
    go = p[:, :, 3 * width:4 * width]
    x1 = jnp.log(lb)
    x2 = jnp.log1p(-lb) + _log_sigmoid(fp)
    log_f = jnp.maximum(x1, x2) + jnp.log1p(jnp.exp(-jnp.abs(x1 - x2)))
    key = (1.0 - lb) * jax.nn.sigmoid(-fp)

    row = _iota((chunk, chunk), 0)
    col = _iota((chunk, chunk), 1)
    incl = jnp.broadcast_to(jnp.where(col <= row, 1.0, 0.0).astype(BF16), (bb, chunk, chunk))
    b = _mm(incl, log_f, BNN, pb=3)
    b_end = b[:, chunk - 1:chunk, :]

    st0 = st[...]
    inter = mm(q * jnp.exp(b), st0, BNT)
    diff = b[:, None, :, :] - b[:, :, None, :]
    shape4 = (bb, chunk, chunk, width)
    pair = jnp.where(_iota(shape4, 1) <= _iota(shape4, 2),
                     jnp.exp(jnp.minimum(diff, 0.0)) * q[:, None, :, :] * key[:, :, None, :], 0.0)
    att = _mm(pair.reshape(bb * chunk * chunk, width), ones_bd, pa=2).reshape(shape4)
    intra = jnp.sum(att * iv[:, :, None, :], axis=1)
    st[...] = st0 * jnp.exp(b_end) + jnp.where(same_head, mm(iv, key * jnp.exp(b_end - b), BTN), 0.0)

    o = inter + intra
    ms = _mm((o * o).reshape(bb * chunk, width), ones_bd, pa=2).reshape(o.shape) * (1.0 / HEAD_DIM)
    o_ref[...] = o * lax.rsqrt(ms + NORM_EPS) * ng_ref[...] * jax.nn.sigmoid(go)

    @pl.when(c == pl.num_programs(1) - 1)
    def _():
        for s in range(bb):
            for h in range(n_heads):
                lanes = slice(h * HEAD_DIM, (h + 1) * HEAD_DIM)
                s_ref[s, h] = st[s, lanes, lanes].T


def _hgrn(p_hg, s0, lb_logits, norm_g_tiled, layer, passes):
    b, t, cols = p_hg.shape
    n_heads = s0.shape[1]
    width = n_heads * HEAD_DIM
    chunk = _divisor(math.gcd(t, 64), 16)
    bb = _divisor(b, HGRN_SEQS, 1)
    depth = lb_logits.shape[0]
    return pl.pallas_call(
        functools.partial(_hgrn_kernel, chunk=chunk, n_heads=n_heads, layer=layer, passes=passes, bb=bb),
        grid=(b // bb, t // chunk),
        in_specs=[
            pl.BlockSpec((bb, chunk, cols), lambda bi, ci: (bi, ci, 0)),
            pl.BlockSpec((bb, n_heads, HEAD_DIM, HEAD_DIM), lambda bi, ci: (bi, 0, 0, 0)),
            pl.BlockSpec((depth, width), lambda bi, ci: (0, 0)),
            pl.BlockSpec((1, width), lambda bi, ci: (0, 0)),
        ],
        out_specs=[
            pl.BlockSpec((bb, chunk, width), lambda bi, ci: (bi, ci, 0)),
            pl.BlockSpec((bb, n_heads, HEAD_DIM, HEAD_DIM), lambda bi, ci: (bi, 0, 0, 0)),
        ],
        out_shape=[
            jax.ShapeDtypeStruct((b, t, width), F32),
            jax.ShapeDtypeStruct((b, n_heads, HEAD_DIM, HEAD_DIM), F32),
        ],
        scratch_shapes=[pltpu.VMEM((bb, width, width), F32)],
        compiler_params=_cparams(("parallel", "arbitrary")),
        name="hgrn2",
    )(p_hg, s0, lb_logits, norm_g_tiled)


def _merge_kernel(x_ref, gt_ref, sh_ref, sc_ref, osb_ref, orw_ref, ohg_ref, pg_ref, wsb_ref, wrw_ref, whg_ref,
                  wout_ref, ng_ref, rw_ref, rb_ref, xo_ref, h_ref, idx_ref, prob_ref, *, rows, d, passes):
    mm = functools.partial(_mm, pa=passes, pb=passes)

    def flat(ref):
        val = ref[...]
        return val.reshape(rows, val.shape[-1])

    gates = jax.nn.sigmoid(flat(pg_ref))
    merged = (gates[:, 0:d] * mm(flat(osb_ref), wsb_ref[...])
              + gates[:, d:2 * d] * mm(flat(orw_ref), wrw_ref[...])
              + gates[:, 2 * d:3 * d] * mm(flat(ohg_ref), whg_ref[...]))
    upd = mm(merged, wout_ref[...]).reshape(x_ref.shape)
    x = x_ref[...] + gt_ref[...] * upd
    xo_ref[...] = x
    ms = jnp.mean(x * x, axis=-1, keepdims=True)
    y = x * lax.rsqrt(ms + NORM_EPS) * ng_ref[...]
    h3 = y * (1.0 + sc_ref[...]) + sh_ref[...]
    h_ref[...] = h3.astype(h_ref.dtype)
    h = h3.reshape(rows, d)
    logits = _mm(h, rw_ref[...], pa=2, pb=2) + rb_ref[...]
    n_exp = logits.shape[-1]
    lane = _iota(logits.shape, 1)
    work = logits
    picks, vals = [], []
    for _ in range(TOP_K):
        m = jnp.max(work, axis=-1, keepdims=True)
        idx = jnp.min(jnp.where(work == m, lane, n_exp), axis=-1, keepdims=True)
        picks.append(idx)
        vals.append(m)
        work = jnp.where(lane == idx, -jnp.inf, work)
    exps = [jnp.exp(vv - vals[0]) for vv in vals]
    denom = exps[0]
    for ee in exps[1:]:
        denom = denom + ee
    out_lane = _iota((rows, LANE), 1)
    idx_out = jnp.zeros((rows, LANE), jnp.int32)
    prob_out = jnp.zeros((rows, LANE), F32)
    for kk, (idx, ee) in enumerate(zip(picks, exps)):
        idx_out = jnp.where(out_lane == kk, idx, idx_out)
        prob_out = jnp.where(out_lane == kk, ee / denom, prob_out)
    idx_ref[...] = idx_out.reshape(idx_ref.shape)
    prob_ref[...] = prob_out.reshape(prob_ref.shape)


def _merge(x3, gt, sh, sc, o_sb, o_rw, o_hg, p_gate, w_sb, w_rw, w_hg, w_out, norm_g, router_w, router_b,
           passes):
    b, t, d = x3.shape
    bb, tt = _row_tiles(b, t, 256)
    n_exp = router_w.shape[1]
    tile = lambda w: pl.BlockSpec((bb, tt, w), lambda i, j: (i, j, 0))
    per_seq = pl.BlockSpec((bb, 1, d), lambda i, j: (i, 0, 0))
    whole = lambda a: pl.BlockSpec(a.shape, lambda i, j: (0,) * a.ndim)
    ng = norm_g.reshape(1, 1, d)
    rb = router_b.reshape(1, n_exp)
    return pl.pallas_call(
        functools.partial(_merge_kernel, rows=bb * tt, d=d, passes=passes),
        grid=(b // bb, t // tt),
        in_specs=[tile(d), per_seq, per_seq, per_seq, tile(o_sb.shape[-1]), tile(o_rw.shape[-1]),
                  tile(o_hg.shape[-1]), tile(p_gate.shape[-1]), whole(w_sb), whole(w_rw), whole(w_hg),
                  whole(w_out), whole(ng), whole(router_w), whole(rb)],
        out_specs=[tile(d), tile(d), tile(LANE), tile(LANE)],
        out_shape=[jax.ShapeDtypeStruct((b, t, d), F32), jax.ShapeDtypeStruct((b, t, d), F32),
                   jax.ShapeDtypeStruct((b, t, LANE), jnp.int32), jax.ShapeDtypeStruct((b, t, LANE), F32)],
        compiler_params=_cparams(("parallel", "parallel")),
        name="merge",
    )(x3, gt, sh, sc, o_sb, o_rw, o_hg, p_gate, w_sb, w_rw, w_hg, w_out, ng, router_w, rb)


MOE_TILE = 256
RANK_TILE = 512
PAIR_TILE = 256


def _rank_kernel(e_ref, rank_ref, count_ref, carry, *, n_exp):
    i = pl.program_id(0)

    @pl.when(i == 0)
    def _():
        carry[...] = jnp.zeros_like(carry)

    e = e_ref[0]
    n = e.shape[1]
    hit = e == _iota((n_exp, n), 0)
    onehot = jnp.where(hit, 1.0, 0.0)
    earlier = jnp.where(_iota((n, n), 0) < _iota((n, n), 1), 1.0, 0.0).astype(BF16)
    before = lax.dot_general(onehot.astype(BF16), earlier, NN, preferred_element_type=F32) + carry[...]
    rank_ref[0] = jnp.sum(jnp.where(hit, before, 0.0), axis=0, keepdims=True).astype(jnp.int32)
    carry[...] = carry[...] + jnp.sum(onehot, axis=1, keepdims=True)
    count_ref[...] = carry[...].astype(jnp.int32)


def _rank(e_flat, n_exp):
    p = e_flat.shape[0]
    tile = _divisor(p, RANK_TILE, LANE)
    rank, count = pl.pallas_call(
        functools.partial(_rank_kernel, n_exp=n_exp),
        grid=(p // tile,),
        in_specs=[pl.BlockSpec((1, 1, tile), lambda i: (i, 0, 0))],
        out_specs=[pl.BlockSpec((1, 1, tile), lambda i: (i, 0, 0)), pl.BlockSpec((n_exp, 1), lambda i: (0, 0))],
        out_shape=[jax.ShapeDtypeStruct((p // tile, 1, tile), jnp.int32),
                   jax.ShapeDtypeStruct((n_exp, 1), jnp.int32)],
        scratch_shapes=[pltpu.VMEM((n_exp, 1), F32)],
        compiler_params=_cparams(("arbitrary",)),
        name="moe_rank",
    )(e_flat.reshape(p // tile, 1, tile))
    return rank.reshape(p), count.reshape(n_exp)


def _row_copies_wait(rows_ref, sem):
    pltpu.make_async_copy(rows_ref, rows_ref, sem).wait()


def _dispatch_kernel(slot_hbm, h_ref, init_hbm, out_hbm, slot_smem, sem_idx, sem_rows, *, tokens):
    del init_hbm
    i = pl.program_id(0)
    pairs = tokens * TOP_K
    idx_copy = pltpu.make_async_copy(slot_hbm.at[pl.ds(i * pairs, pairs)], slot_smem, sem_idx)
    idx_copy.start()
    idx_copy.wait()

    def issue(r, carry):
        for kk in range(TOP_K):
            slot = slot_smem[r * TOP_K + kk]
            pltpu.make_async_copy(h_ref.at[pl.ds(r, 1), :], out_hbm.at[pl.ds(slot, 1), :],
                                  sem_rows).start(priority=kk % 2)
        return carry

    lax.fori_loop(0, tokens, issue, 0)
    for _ in range(TOP_K):
        _row_copies_wait(h_ref, sem_rows)


def _dispatch(h, slots, n_slots):
    n, d = h.shape
    tokens = _divisor(n, PAIR_TILE)
    return pl.pallas_call(
        functools.partial(_dispatch_kernel, tokens=tokens),
        grid=(n // tokens,),
        in_specs=[
            pl.BlockSpec(memory_space=pl.ANY),
            pl.BlockSpec((tokens, d), lambda i: (i, 0)),
            pl.BlockSpec(memory_space=pl.ANY),
        ],
        out_specs=pl.BlockSpec(memory_space=pl.ANY),
        out_shape=jax.ShapeDtypeStruct((n_slots, d), F32),
        scratch_shapes=[pltpu.SMEM((tokens * TOP_K,), jnp.int32), pltpu.SemaphoreType.DMA(()),
                        pltpu.SemaphoreType.DMA(())],
        input_output_aliases={2: 0},
        compiler_params=_cparams(("arbitrary",)),
        name="moe_dispatch",
    )(slots, h, jnp.zeros((n_slots, d), F32))


def _experts_kernel(te_ref, x_ref, w1_ref, b1_ref, w2_ref, b2_ref, o_ref, w1b, w2b, *, d_exp, n_used_idx):
    i = pl.program_id(0)
    expert = te_ref[i]
    fresh = jnp.logical_or(i == 0, expert != te_ref[jnp.maximum(i - 1, 0)])

    @pl.when(fresh)
    def _():
        rows = w1b.shape[0] // 8
        for c in range(8):
            w1b[c * rows:(c + 1) * rows, :] = w1_ref[0, c * rows:(c + 1) * rows, :].astype(BF16)
        rows = w2b.shape[0] // 8
        for c in range(8):
            w2b[c * rows:(c + 1) * rows, :] = w2_ref[0, c * rows:(c + 1) * rows, :].astype(BF16)

    used = i < te_ref[n_used_idx]

    @pl.when(used)
    def _():
        gu = lax.dot_general(x_ref[...].astype(BF16), w1b[...], NN, preferred_element_type=F32) + b1_ref[0]
        glu = jnp.minimum(gu[:, :d_exp], SWIGLU_LIMIT)
        lin = jnp.clip(gu[:, d_exp:], -SWIGLU_LIMIT, SWIGLU_LIMIT)
        act = glu * jax.nn.sigmoid(SWIGLU_ALPHA * glu) * (lin + 1.0)
        o_ref[...] = lax.dot_general(act.astype(BF16), w2b[...], NN, preferred_element_type=F32) + b2_ref[0]

    @pl.when(jnp.logical_not(used))
    def _():
        o_ref[...] = jnp.zeros_like(o_ref)


def _experts(x_slots, tile_expert, w1, b1, w2, b2, first_expert):
    s, d = x_slots.shape
    n_exp, _, two_de = w1.shape
    d_exp = two_de // 2
    n_tiles = s // MOE_TILE
    grid_spec = pltpu.PrefetchScalarGridSpec(
        num_scalar_prefetch=1,
        grid=(n_tiles,),
        in_specs=[
            pl.BlockSpec((MOE_TILE, d), lambda i, te: (i, 0)),
            pl.BlockSpec((1, d, two_de), lambda i, te: (first_expert + te[i], 0, 0)),
            pl.BlockSpec((1, 1, two_de), lambda i, te: (first_expert + te[i], 0, 0)),
            pl.BlockSpec((1, d_exp, d), lambda i, te: (first_expert + te[i], 0, 0)),
            pl.BlockSpec((1, 1, d), lambda i, te: (first_expert + te[i], 0, 0)),
        ],
        out_specs=pl.BlockSpec((MOE_TILE, d), lambda i, te: (i, 0)),
        scratch_shapes=[pltpu.VMEM((d, two_de), BF16), pltpu.VMEM((d_exp, d), BF16)],
    )
    return pl.pallas_call(
        functools.partial(_experts_kernel, d_exp=d_exp, n_used_idx=n_tiles),
        grid_spec=grid_spec,
        out_shape=jax.ShapeDtypeStruct((s, d), F32),
        compiler_params=_cparams(("arbitrary",)),
        name="moe_experts",
    )(tile_expert, x_slots, w1, b1.reshape(n_exp, 1, two_de), w2, b2.reshape(n_exp, 1, d))


def _combine_kernel(slot_hbm, y_hbm, prob_ref, o_ref, slot_smem, sem_idx, sem_rows, rows_buf, *, tokens, base):
    i = pl.program_id(0)
    pairs = tokens * TOP_K
    idx_copy = pltpu.make_async_copy(slot_hbm.at[pl.ds(base + i * pairs, pairs)], slot_smem, sem_idx)
    idx_copy.start()
    idx_copy.wait()

    def issue(r, carry):
        for kk in range(TOP_K):
            slot = slot_smem[r * TOP_K + kk]
            pltpu.make_async_copy(y_hbm.at[pl.ds(slot, 1), :], rows_buf.at[kk, pl.ds(r, 1), :],
                                  sem_rows).start(priority=kk % 2)
        return carry

    lax.fori_loop(0, tokens, issue, 0)
    _row_copies_wait(rows_buf, sem_rows)
    prob = prob_ref[...]
    acc = prob[:, 0:1] * rows_buf[0]
    for kk in range(1, TOP_K):
        acc = acc + prob[:, kk:kk + 1] * rows_buf[kk]
    o_ref[...] = acc


def _combine(y_slots, slots, prob, first_token):
    n = prob.shape[0]
    d = y_slots.shape[1]
    tokens = _divisor(n, PAIR_TILE)
    return pl.pallas_call(
        functools.partial(_combine_kernel, tokens=tokens, base=first_token * TOP_K),
        grid=(n // tokens,),
        in_specs=[
            pl.BlockSpec(memory_space=pl.ANY),
            pl.BlockSpec(memory_space=pl.ANY),
            pl.BlockSpec((tokens, LANE), lambda i: (i, 0)),
        ],
        out_specs=pl.BlockSpec((tokens, d), lambda i: (i, 0)),
        out_shape=jax.ShapeDtypeStruct((n, d), F32),
        scratch_shapes=[pltpu.SMEM((tokens * TOP_K,), jnp.int32), pltpu.SemaphoreType.DMA(()),
                        pltpu.SemaphoreType.DMA(()), pltpu.VMEM((TOP_K, tokens, d), F32)],
        compiler_params=_cparams(("arbitrary",)),
        name="moe_combine",
    )(slots, y_slots, prob)


def _moe_plan(idx, n_exp):
    n = idx.shape[0]
    pairs = n * TOP_K
    e_flat = idx.reshape(pairs)
    rank, count = _rank(e_flat, n_exp)
    padded = ((count + MOE_TILE - 1) // MOE_TILE) * MOE_TILE
    group_end = jnp.cumsum(padded)
    group_start = group_end - padded
    slots = (jnp.take(group_start, e_flat) + rank).astype(jnp.int32)
    n_slots = ((pairs + MOE_TILE - 1) // MOE_TILE) * MOE_TILE + n_exp * MOE_TILE
    tile_start = jnp.arange(n_slots // MOE_TILE, dtype=jnp.int32) * MOE_TILE
    owner = jnp.sum((tile_start[:, None] >= group_end[None, :]).astype(jnp.int32), axis=1)
    tile_expert = jnp.minimum(owner, n_exp - 1).astype(jnp.int32)
    tiles_used = (group_end[-1] // MOE_TILE).astype(jnp.int32)
    return slots, jnp.concatenate([tile_expert, tiles_used[None]]), n_slots


def _pad_cols(a, n):
    return jnp.pad(a, [(0, 0)] * (a.ndim - 1) + [(0, n - a.shape[-1])])


def _pad_rows(a, n):
    return jnp.pad(a, [(0, n - a.shape[0]), (0, 0)])


def _rw_layout(a, rw_width):
    o1 = 3 * rw_width
    o2 = o1 + RW_DECAY_LORA
    o3 = o2 + RW_AAA_LORA
    return jnp.concatenate([
        a[..., :o1], _pad_cols(a[..., o1:o2], LANE), _pad_cols(a[..., o2:o3], LANE),
        _pad_cols(a[..., o3:], 2 * LANE)], axis=-1)


def _rw_unlayout(a, rw_width):
    o1 = 3 * rw_width
    return jnp.concatenate([
        a[..., :o1], a[..., o1:o1 + RW_DECAY_LORA], a[..., o1 + LANE:o1 + LANE + RW_AAA_LORA],
        a[..., o1 + 2 * LANE:o1 + 2 * LANE + RW_GATE_LORA]], axis=-1)


def _mixer_half(grp, l, w, mod_l):
    x = grp["x"]
    b, t, d = x.shape
    sb_w, rw_w = w["sb_w"], w["rw_w"]
    m = mod_l.reshape(b, 1, 6 * d)
    sh_a, sc_a, gt_a, sh_f, sc_f, gt_f = [m[:, :, i * d:(i + 1) * d] for i in range(6)]
    passes = w["passes"]
    delta = grp["delta"]
    x_new, h = _norm(x, delta, grp["gt_prev"], sh_a, sc_a, w["norm_attn_g"], emit_x=delta is not None,
                     h_dtype=BF16 if passes == 1 else F32)
    if x_new is not None:
        x = x_new
    hf = h.reshape(b * t, d)
    q, k, v = [a.reshape(b, t, sb_w) for a in _proj(hf, w["w_sb"], (sb_w, sb_w, sb_w), passes)]
    (p_rw,) = _proj(hf, w["w_rw"], (w["w_rw"].shape[1],), passes)
    (p_hg,) = _proj(hf, w["w_hg"], (w["w_hg"].shape[1],), passes)
    (p_gate,) = _proj(hf, w["w_gate"], (w["w_gate"].shape[1],), passes)
    p_rw = p_rw.reshape(b, t, -1)
    p_hg = p_hg.reshape(b, t, -1)
    p_gate = p_gate.reshape(b, t, -1)
    o_sb = grp["attn"](l, q, k, v, w["sb_bias"])
    o_rw, s_rw = _rwkv(p_rw, _rw_layout(grp["rw_shift"][l], rw_w).reshape(b, 1, -1), grp["rw_state"][l], w["rw"],
                       passes)
    o_hg, s_hg = _hgrn(p_hg, grp["hg_state"][l], w["hg_lb_logits"], w["hg_norm_g"], l, passes)
    x, h2, idx, prob = _merge(x, gt_a, sh_f, sc_f, o_sb, o_rw, o_hg, p_gate, w["w_branch_sb"], w["w_branch_rw"],
                              w["w_branch_hg"], w["w_out"], w["norm_ffn_g"], w["router_w"], w["router_b"], passes)
    grp["x"] = x
    grp["gt_prev"] = gt_f
    grp["ks"].append(k.reshape(b, t, sb_w // HEAD_DIM, HEAD_DIM))
    grp["vs"].append(v.reshape(b, t, sb_w // HEAD_DIM, HEAD_DIM))
    grp["rws"].append(s_rw)
    grp["shs"].append(_rw_unlayout(p_rw[:, -1], rw_w))
    grp["hgs"].append(s_hg)
    return h2.reshape(b * t, d), idx.reshape(b * t, LANE)[:, :TOP_K], prob.reshape(b * t, LANE)


def _run_layers(groups, mods, lw, moe, depth):
    n_exp = moe["w1"].shape[0] // depth
    for l in range(depth):
        halves = [_mixer_half(grp, l, lw[l], mod[l]) for grp, mod in zip(groups, mods)]
        h_all = jnp.concatenate([hh[0] for hh in halves], axis=0)
        idx_all = jnp.concatenate([hh[1] for hh in halves], axis=0)
        slots, tile_expert, n_slots = _moe_plan(idx_all, n_exp)
        x_slots = _dispatch(h_all, slots, n_slots)
        y_slots = _experts(x_slots, tile_expert, moe["w1"], moe["b1"], moe["w2"], moe["b2"], l * n_exp)
        first = 0
        for grp, hh in zip(groups, halves):
            b, t, d = grp["x"].shape
            grp["delta"] = _combine(y_slots, slots, hh[2], first).reshape(b, t, d)
            first += b * t
    outs = []
    for grp in groups:
        b, t, d = grp["x"].shape
        zeros = jnp.zeros((b, 1, d), F32)
        _, y = _norm(grp["x"], grp["delta"], grp["gt_prev"], zeros, zeros, lw[0]["final_norm_g"], emit_x=False,
                     h_dtype=F32)
        outs.append((y, jnp.stack(grp["ks"]), jnp.stack(grp["vs"]), jnp.stack(grp["rws"]), jnp.stack(grp["shs"]),
                     jnp.stack(grp["hgs"])))
    return outs


def kernel(x_prompt, x_sample, cache_sb_k, cache_sb_v, state_rwkv, state_rwkv_shift, state_hgrn, page_table,
           c_prompt, c_sample, w_ada, b_ada, norm_attn_g, norm_ffn_g, w_in, sb_bias, rw_mu, rw_w0, rw_w_up, rw_a0,
           rw_a_up, rw_g_up, rw_k_k, rw_k_a, rw_r_k, rw_ln_w, rw_ln_b, hg_lb_logits, hg_norm_g, w_branch_sb,
           w_branch_rw, w_branch_hg, w_out, router_w, router_b, moe_w1, moe_b1, moe_w2, moe_b2, final_norm_g):
    depth, d, _ = w_in.shape
    bp = x_prompt.shape[0]
    db = x_sample.shape[0]
    sb_w = w_branch_sb.shape[1]
    rw_w = w_branch_rw.shape[1]
    hg_w = w_branch_hg.shape[1]
    rw_heads = rw_w // HEAD_DIM
    hg_heads = hg_w // HEAD_DIM
    rw_cols = 3 * rw_w + RW_DECAY_LORA + RW_AAA_LORA + RW_GATE_LORA
    o_rw = 3 * sb_w
    o_hg = o_rw + rw_cols
    o_gate = o_hg + 4 * hg_w

    mod = _ada(jnp.concatenate([c_prompt, c_sample], axis=0), w_ada, b_ada)

    lw = []
    for l in range(depth):
        wl = w_in[l]
        passes = 2 if l + 1 < depth else 1
        wdt = BF16 if passes == 1 else F32
        lw.append(dict(
            sb_w=sb_w, rw_w=rw_w, passes=passes,
            w_sb=wl[:, :o_rw].astype(wdt),
            w_rw=_rw_layout(wl[:, o_rw:o_hg], rw_w).astype(wdt),
            w_hg=wl[:, o_hg:o_gate].astype(wdt),
            w_gate=wl[:, o_gate:].astype(wdt),
            norm_attn_g=norm_attn_g[l], norm_ffn_g=norm_ffn_g[l], final_norm_g=final_norm_g,
            sb_bias=sb_bias[l],
            rw=dict(
                mu=_rw_layout(rw_mu[l][None, :], rw_w), w0=rw_w0[l][None, :],
                w_up=_pad_rows(rw_w_up[l], LANE), a0=rw_a0[l][None, :], a_up=_pad_rows(rw_a_up[l], LANE),
                g_up=_pad_rows(rw_g_up[l], 2 * LANE), k_k=rw_k_k[l][None, :], k_a=rw_k_a[l][None, :],
                r_k=rw_r_k[l].reshape(1, rw_w), ln_w=rw_ln_w[l][None, :], ln_b=rw_ln_b[l][None, :]),
            hg_lb_logits=hg_lb_logits,
            hg_norm_g=jnp.tile(hg_norm_g[l], hg_heads)[None, :],
            w_branch_sb=w_branch_sb[l].astype(wdt), w_branch_rw=w_branch_rw[l].astype(wdt),
            w_branch_hg=w_branch_hg[l].astype(wdt), w_out=w_out[l].astype(wdt),
            router_w=router_w[l], router_b=router_b[l],
        ))
    n_exp = moe_w1.shape[1]
    moe = dict(w1=moe_w1.reshape((depth * n_exp,) + moe_w1.shape[2:]), b1=moe_b1.reshape(depth * n_exp, -1),
               w2=moe_w2.reshape((depth * n_exp,) + moe_w2.shape[2:]), b2=moe_b2.reshape(depth * n_exp, -1))

    def prompt_attn(l, q, k, v, bias):
        del l
        return _sb_prompt(q, k, v, bias)

    def sample_attn(l, q, k, v, bias):
        return _sb_sample(q, k, v, cache_sb_k, cache_sb_v, page_table, bias, l)

    zeros_rw = jnp.zeros((depth, bp, rw_heads, HEAD_DIM, HEAD_DIM), F32)
    zeros_sh = jnp.zeros((depth, bp, rw_cols), F32)
    zeros_hg = jnp.zeros((depth, bp, hg_heads, HEAD_DIM, HEAD_DIM), F32)

    def group(x, attn, rw_state, rw_shift, hg_state):
        return dict(x=x, attn=attn, rw_state=rw_state, rw_shift=rw_shift, hg_state=hg_state, delta=None,
                    gt_prev=None, ks=[], vs=[], rws=[], shs=[], hgs=[])

    groups = [group(x_prompt, prompt_attn, zeros_rw, zeros_sh, zeros_hg),
              group(x_sample, sample_attn, state_rwkv, state_rwkv_shift, state_hgrn)]
    out_p, out_s = _run_layers(groups, [mod[:, :bp], mod[:, bp:]], lw, moe, depth)
    return (out_p[0], out_s[0]) + tuple(out_p[1:]) + tuple(out_s[1:])
```

```python
import functools
import math

import jax
import jax.numpy as jnp
from jax import lax
from jax.experimental import pallas as pl
from jax.experimental.pallas import tpu as pltpu

F32 = jnp.float32
BF16 = jnp.bfloat16

HEAD_DIM = 64
PAGE_SIZE = 128
TOP_K = 4
SWIGLU_LIMIT = 7.0
SWIGLU_ALPHA = 1.702
NORM_EPS = 1e-5
RW_GN_EPS = HEAD_DIM * 1e-5
RW_DECAY_LORA = 64
RW_AAA_LORA = 64
RW_GATE_LORA = 160
LANE = 128
VMEM_LIMIT = 48 * 1024 * 1024

NN = (((1,), (0,)), ((), ()))
NT = (((1,), (1,)), ((), ()))
TN = (((0,), (0,)), ((), ()))


def _cparams(sem):
    return pltpu.CompilerParams(dimension_semantics=sem, vmem_limit_bytes=VMEM_LIMIT)


def _split(x, n):
    parts = []
    r = x
    for i in range(n):
        p = r.astype(BF16)
        parts.append(p)
        if i + 1 < n:
            r = r - p.astype(F32)
    return parts


MXU_DEPTH = 256


def _mm(a, b, dims=NN, pa=1, pb=1):
    a_parts = _split(a, pa) if a.dtype != BF16 else [a]
    b_parts = _split(b, pb) if b.dtype != BF16 else [b]
    order = max(len(a_parts), len(b_parts))
    terms = [(ap, bp) for i, ap in enumerate(a_parts) for j, bp in enumerate(b_parts) if i + j < order]
    (ca,), (cb,) = dims[0]
    if len(terms) > 1 and len(terms) * a.shape[ca] <= MXU_DEPTH:
        a_cat = jnp.concatenate([t[0] for t in terms], axis=ca)
        b_cat = jnp.concatenate([t[1] for t in terms], axis=cb)
        return lax.dot_general(a_cat, b_cat, dims, preferred_element_type=F32)
    out = None
    for ap, bp in terms:
        t = lax.dot_general(ap, bp, dims, preferred_element_type=F32)
        out = t if out is None else out + t
    return out


def _log_sigmoid(z):
    return jnp.minimum(z, 0.0) - jnp.log1p(jnp.exp(-jnp.abs(z)))


def _sb_logs(z):
    soft = jnp.log(1.0 + jnp.exp(-jnp.abs(z)))
    return jnp.minimum(z, 0.0) - soft, -jnp.maximum(z, 0.0) - soft


def _softplus(z):
    return jnp.maximum(z, 0.0) + jnp.log1p(jnp.exp(-jnp.abs(z)))


def _iota(shape, dim):
    return lax.broadcasted_iota(jnp.int32, shape, dim)


def _divisor(n, target, mult=8):
    if n <= target:
        return n
    for d in range(target, 0, -1):
        if n % d == 0 and d % mult == 0:
            return d
    return n


def _row_tiles(batch, seq, target):
    if seq >= target:
        return 1, _divisor(seq, target)
    bb = 1
    for d in range(1, batch + 1):
        if batch % d == 0 and d * seq <= target:
            bb = d
    return bb, seq


def _ada_kernel(c_ref, w_ref, b_ref, o_ref):
    c = c_ref[...]
    act = c * jax.nn.sigmoid(c)
    o_ref[0] = _mm(act, w_ref[0], pa=2, pb=2) + b_ref[0]


def _ada(c_all, w_ada, b_ada):
    depth, d, n6 = w_ada.shape
    rows = c_all.shape[0]
    tn = _divisor(n6, 1536, LANE)
    return pl.pallas_call(
        _ada_kernel,
        grid=(depth, n6 // tn),
        in_specs=[
            pl.BlockSpec((rows, d), lambda l, j: (0, 0)),
            pl.BlockSpec((1, d, tn), lambda l, j: (l, 0, j)),
            pl.BlockSpec((1, 1, tn), lambda l, j: (l, 0, j)),
        ],
        out_specs=pl.BlockSpec((1, rows, tn), lambda l, j: (l, 0, j)),
        out_shape=jax.ShapeDtypeStruct((depth, rows, n6), F32),
        compiler_params=_cparams(("parallel", "parallel")),
        name="ada",
    )(c_all, w_ada, b_ada.reshape(depth, 1, n6))


def _norm_kernel(*refs, has_delta, emit_x):
    it = iter(refs)
    x_ref = next(it)
    if has_delta:
        d_ref = next(it)
        gt_ref = next(it)
    sh_ref = next(it)
    sc_ref = next(it)
    g_ref = next(it)
    if emit_x:
        xo_ref = next(it)
    h_ref = next(it)
    x = x_ref[...]
    if has_delta:
        x = x + gt_ref[...] * d_ref[...]
    if emit_x:
        xo_ref[...] = x
    ms = jnp.mean(x * x, axis=-1, keepdims=True)
    y = x * lax.rsqrt(ms + NORM_EPS) * g_ref[...]
    h_ref[...] = (y * (1.0 + sc_ref[...]) + sh_ref[...]).astype(h_ref.dtype)


def _norm(x3, delta3, gt, sh, sc, g, *, emit_x, h_dtype):
    b, t, d = x3.shape
    bb, tt = _row_tiles(b, t, 512)
    tile = pl.BlockSpec((bb, tt, d), lambda i, j: (i, j, 0))
    per_seq = pl.BlockSpec((bb, 1, d), lambda i, j: (i, 0, 0))
    has_delta = delta3 is not None
    args, in_specs = [x3], [tile]
    if has_delta:
        args += [delta3, gt]
        in_specs += [tile, per_seq]
    args += [sh, sc, g.reshape(1, 1, d)]
    in_specs += [per_seq, per_seq, pl.BlockSpec((1, 1, d), lambda i, j: (0, 0, 0))]
    out_shape, out_specs = [], []
    if emit_x:
        out_shape.append(jax.ShapeDtypeStruct((b, t, d), F32))
        out_specs.append(tile)
    out_shape.append(jax.ShapeDtypeStruct((b, t, d), h_dtype))
    out_specs.append(tile)
    outs = pl.pallas_call(
        functools.partial(_norm_kernel, has_delta=has_delta, emit_x=emit_x),
        grid=(b // bb, t // tt),
        in_specs=in_specs,
        out_specs=out_specs,
        out_shape=out_shape,
        compiler_params=_cparams(("parallel", "parallel")),
        name="norm",
    )(*args)
    return outs if emit_x else (None, outs[0])


def _proj_kernel(h_ref, w_ref, *o_refs, widths, chunk, passes):
    h = h_ref[...]
    col = 0
    for o_ref, width in zip(o_refs, widths):
        for c0 in range(0, width, chunk):
            c1 = min(c0 + chunk, width)
            o_ref[:, c0:c1] = _mm(h, w_ref[:, col + c0:col + c1], pa=passes, pb=passes)
        col += width


def _proj(h, w, widths, passes):
    n, k = h.shape
    tm = _divisor(n, 512 if passes == 1 else 256)
    return pl.pallas_call(
        functools.partial(_proj_kernel, widths=tuple(widths), chunk=512, passes=passes),
        grid=(n // tm,),
        in_specs=[
            pl.BlockSpec((tm, k), lambda i: (i, 0)),
            pl.BlockSpec((k, w.shape[1]), lambda i: (0, 0)),
        ],
        out_specs=[pl.BlockSpec((tm, wd), lambda i: (i, 0)) for wd in widths],
        out_shape=[jax.ShapeDtypeStruct((n, wd), F32) for wd in widths],
        compiler_params=_cparams(("parallel",)),
        name="proj",
    )(h, w)


def _sb_weights(z, run, upper, readable):
    log_beta, log_skip = _sb_logs(z)
    if readable is not None:
        log_skip = jnp.where(readable, log_skip, 0.0)
    later = _mm(log_skip, upper, pa=2) + run
    w = jnp.exp(log_beta + later)
    if readable is not None:
        w = jnp.where(readable, w, 0.0)
    return w, run + jnp.sum(log_skip, axis=-1, keepdims=True)


SB_HEADS_PER_LOOP = 4
SB_BLOCK = 256


def _sb_prompt_kernel(bias_ref, q_ref, k_ref, v_ref, o_ref, *, tq, n_heads):
    i = pl.program_id(1)
    scale = HEAD_DIM ** -0.5
    row = _iota((tq, tq), 0)
    col = _iota((tq, tq), 1)
    upper = jnp.where(row > col, 1.0, 0.0).astype(BF16)
    diag_readable = col < row
    for h0 in range(0, n_heads, SB_HEADS_PER_LOOP):
        group = list(range(h0, min(h0 + SB_HEADS_PER_LOOP, n_heads)))
        lanes = [slice(h * HEAD_DIM, (h + 1) * HEAD_DIM) for h in group]
        qs = [(q_ref[0, :, ln] * scale).astype(BF16) for ln in lanes]
        biases = [bias_ref[h] for h in group]

        def block(j, g, run, readable):
            start = pl.multiple_of(j * tq, tq)
            kh = k_ref[0, pl.ds(start, tq), lanes[g]].astype(BF16)
            vh = v_ref[0, pl.ds(start, tq), lanes[g]].astype(BF16)
            z = lax.dot_general(qs[g], kh, NT, preferred_element_type=F32) + biases[g]
            w, run = _sb_weights(z, run, upper, readable)
            return lax.dot_general(w.astype(BF16), vh, NN, preferred_element_type=F32), run

        carry = []
        for g in range(len(group)):
            carry.extend(block(i, g, jnp.zeros((tq, 1), F32), diag_readable))

        def body(jj, carry):
            new = []
            for g in range(len(group)):
                out, run = block(i - 1 - jj, g, carry[2 * g + 1], None)
                new.extend((carry[2 * g] + out, run))
            return tuple(new)

        carry = lax.fori_loop(0, i, body, tuple(carry))
        for g in range(len(group)):
            o_ref[0, :, lanes[g]] = carry[2 * g]


def _sb_prompt(q, k, v, bias):
    b, t, width = q.shape
    n_heads = width // HEAD_DIM
    tq = _divisor(t, SB_BLOCK)
    return pl.pallas_call(
        functools.partial(_sb_prompt_kernel, tq=tq, n_heads=n_heads),
        grid=(b, t // tq),
        in_specs=[
            pl.BlockSpec(memory_space=pltpu.SMEM),
            pl.BlockSpec((1, tq, width), lambda bi, i: (bi, i, 0)),
            pl.BlockSpec((1, t, width), lambda bi, i: (bi, 0, 0)),
            pl.BlockSpec((1, t, width), lambda bi, i: (bi, 0, 0)),
        ],
        out_specs=pl.BlockSpec((1, tq, width), lambda bi, i: (bi, i, 0)),
        out_shape=jax.ShapeDtypeStruct((b, t, width), F32),
        compiler_params=_cparams(("parallel", "arbitrary")),
        name="sb_prompt",
    )(bias, q, k, v)


SB_PAGES_PER_STEP = 8


def _sb_sample_kernel(pt_ref, bias_ref, q_ref, kn_ref, vn_ref, *refs, pps, n_heads, t_new):
    del pt_ref
    k_refs = refs[:pps]
    v_refs = refs[pps:2 * pps]
    o_ref, acc_ref, run_ref = refs[2 * pps:]
    step = pl.program_id(1)
    width = n_heads * HEAD_DIM
    cols = n_heads * t_new
    scale = HEAD_DIM ** -0.5
    q = q_ref[0] * scale
    q_exp = jnp.concatenate([q] * n_heads, axis=0)
    head_of_row = _iota((cols, width), 0) // t_new
    head_of_lane = _iota((cols, width), 1) // HEAD_DIM
    q_exp = jnp.where(head_of_row == head_of_lane, q_exp, 0.0).astype(BF16)
    bias_col = jnp.zeros((cols, 1), F32)
    row_head = _iota((cols, 1), 0) // t_new
    for h in range(n_heads):
        bias_col = jnp.where(row_head == h, bias_ref[h], bias_col)

    def later_matrix(nk):
        return jnp.where(_iota((nk, nk), 0) > _iota((nk, nk), 1), 1.0, 0.0).astype(BF16)

    @pl.when(step == 0)
    def _():
        readable = _iota((cols, t_new), 1) < _iota((cols, t_new), 0) % t_new
        z = lax.dot_general(q_exp, kn_ref[0].astype(BF16), NT, preferred_element_type=F32) + bias_col
        log_beta, log_skip = _sb_logs(z)
        log_skip = jnp.where(readable, log_skip, 0.0)
        later = _mm(log_skip, later_matrix(t_new), pa=2)
        w = jnp.where(readable, jnp.exp(log_beta + later), 0.0)
        acc_ref[...] = lax.dot_general(w.astype(BF16), vn_ref[0].astype(BF16), NN, preferred_element_type=F32)
        run_ref[...] = jnp.sum(log_skip, axis=1, keepdims=True)

    def page_t(ref):
        return ref[0, 0].reshape(width, PAGE_SIZE).astype(BF16)

    k_t = jnp.concatenate([page_t(r) for r in k_refs], axis=1)
    v_t = jnp.concatenate([page_t(r) for r in v_refs], axis=1)
    z = lax.dot_general(q_exp, k_t, NN, preferred_element_type=F32) + bias_col
    log_beta, log_skip = _sb_logs(z)

    def by_page(x):
        return jnp.concatenate([x[:, j * PAGE_SIZE:(j + 1) * PAGE_SIZE] for j in range(pps)], axis=0)

    log_beta = by_page(log_beta)
    log_skip = by_page(log_skip)
    later = _mm(log_skip, later_matrix(PAGE_SIZE), pa=2)
    page_sum = jnp.sum(log_skip, axis=1, keepdims=True)
    run = run_ref[...]
    offsets = []
    for j in range(pps):
        offsets.append(run)
        run = run + page_sum[j * cols:(j + 1) * cols]
    run_ref[...] = run
    w = jnp.exp(log_beta + later + jnp.concatenate(offsets, axis=0)).astype(BF16)
    w = jnp.concatenate([w[j * cols:(j + 1) * cols] for j in range(pps)], axis=1)
    acc_ref[...] += lax.dot_general(w, v_t, NT, preferred_element_type=F32)

    @pl.when(step == pl.num_programs(1) - 1)
    def _():
        acc = acc_ref[...]
        lane_head = _iota((t_new, width), 1) // HEAD_DIM
        res = jnp.zeros((t_new, width), F32)
        for h in range(n_heads):
            res = res + jnp.where(lane_head == h, acc[h * t_new:(h + 1) * t_new, :], 0.0)
        o_ref[0] = res


def _sb_sample(q, k_new, v_new, cache_k, cache_v, page_table, bias, layer):
    b, t_new, width = q.shape
    n_pages = page_table.shape[1]
    n_heads = width // HEAD_DIM
    pps = SB_PAGES_PER_STEP if n_pages % SB_PAGES_PER_STEP == 0 else n_pages

    def page_spec(j):
        def index(bi, s, pt):
            return (layer, pt[bi * n_pages + (n_pages - 1 - (s * pps + j))], 0, 0, 0)
        return pl.BlockSpec((1, 1, n_heads, HEAD_DIM, PAGE_SIZE), index)

    new_spec = pl.BlockSpec((1, t_new, width), lambda bi, s, pt: (bi, 0, 0))
    grid_spec = pltpu.PrefetchScalarGridSpec(
        num_scalar_prefetch=1,
        grid=(b, n_pages // pps),
        in_specs=[pl.BlockSpec(memory_space=pltpu.SMEM), new_spec, new_spec, new_spec]
        + [page_spec(j) for j in range(pps)] + [page_spec(j) for j in range(pps)],
        out_specs=new_spec,
        scratch_shapes=[pltpu.VMEM((n_heads * t_new, width), F32), pltpu.VMEM((n_heads * t_new, 1), F32)],
    )
    return pl.pallas_call(
        functools.partial(_sb_sample_kernel, pps=pps, n_heads=n_heads, t_new=t_new),
        grid_spec=grid_spec,
        out_shape=jax.ShapeDtypeStruct((b, t_new, width), F32),
        compiler_params=_cparams(("parallel", "arbitrary")),
        name="sb_sample",
    )(page_table.reshape(-1), bias, q, k_new, v_new, *([cache_k] * pps), *([cache_v] * pps))


RWKV_SEQS = 8

BNN = (((2,), (1,)), ((0,), (0,)))
BNT = (((2,), (2,)), ((0,), (0,)))
BTN = (((1,), (1,)), ((0,), (0,)))


def _unit_lower_inverse(n_mat, size):
    row = _iota((size, size), 0)
    col = _iota((size, size), 1)
    eye = jnp.where(row == col, 1.0, 0.0)
    inv = eye + jnp.where(row // 2 == col // 2, n_mat, 0.0)
    s = 2
    while s < size:
        off = jnp.where((row // (2 * s) == col // (2 * s)) & (row // s != col // s), n_mat, 0.0)
        inv = inv + _mm(_mm(inv, off, BNN, pa=2, pb=2), inv, BNN, pa=2, pb=2)
        s *= 2
    return inv


def _rwkv_kernel(p_ref, shift_ref, s0_ref, mu_ref, w0_ref, wup_ref, a0_ref, aup_ref, gup_ref, kk_ref, ka_ref,
                 rk_ref, lnw_ref, lnb_ref, o_ref, s_ref, state, prev, *, chunk, n_heads, passes, bb):
    c = pl.program_id(1)
    width = n_heads * HEAD_DIM
    groups = bb * n_heads
    mm = functools.partial(_mm, pa=passes, pb=passes)

    @pl.when(c == 0)
    def _():
        state[...] = s0_ref[...].reshape(groups, HEAD_DIM, HEAD_DIM)
        prev[...] = shift_ref[...]

    p = p_ref[...]
    cols = p.shape[-1]
    rolled = pltpu.roll(p, 1, axis=1)
    prev_rows = jnp.where(_iota(p.shape, 1) == 0, prev[...], rolled)
    prev[...] = p[:, chunk - 1:chunk, :]
    xs = (p + (prev_rows - p) * mu_ref[...]).reshape(bb * chunk, cols)
    r = xs[:, 0:width]
    k = xs[:, width:2 * width]
    v = xs[:, 2 * width:3 * width]
    o1 = 3 * width
    wd = xs[:, o1:o1 + LANE]
    ad = xs[:, o1 + LANE:o1 + 2 * LANE]
    gd = xs[:, o1 + 2 * LANE:o1 + 2 * LANE + 2 * LANE]
    w_log = -_softplus(-(w0_ref[...] + mm(jnp.tanh(wd), wup_ref[...]))) - 0.5
    log_decay = -jnp.exp(w_log)
    a = jax.nn.sigmoid(a0_ref[...] + mm(ad, aup_ref[...]))
    g = mm(jax.nn.sigmoid(gd), gup_ref[...])
    kk_all = k * kk_ref[...]
    k2 = k * (1.0 + (a - 1.0) * ka_ref[...])

    def heads(x2):
        x3 = x2.reshape(bb, chunk, width)
        x4 = jnp.stack([x3[:, :, h * HEAD_DIM:(h + 1) * HEAD_DIM] for h in range(n_heads)], axis=1)
        return x4.reshape(groups, chunk, HEAD_DIM)

    def head_param(ref):
        row = ref[...]
        per_head = jnp.stack([row[:, h * HEAD_DIM:(h + 1) * HEAD_DIM] for h in range(n_heads)], axis=0)
        return jnp.concatenate([per_head] * bb, axis=0)

    row = _iota((chunk, chunk), 0)
    col = _iota((chunk, chunk), 1)
    strict = col < row
    lower = col <= row
    incl = jnp.broadcast_to(jnp.where(lower, 1.0, 0.0).astype(BF16), (groups, chunk, chunk))
    ld = heads(log_decay)
    bh = _mm(incl, ld, BNN, pb=3)
    kkh = heads(kk_all)
    kkh = kkh / jnp.maximum(jnp.sqrt(jnp.sum(kkh * kkh, axis=-1, keepdims=True)), 1e-12)
    rh, kh, vh, ah = heads(r), heads(k2), heads(v), heads(a)
    eb = jnp.exp(bh)
    einv = jnp.exp(-bh)
    al = -kkh * jnp.exp(bh - ld)
    be = kkh * ah * einv
    kb = kh * einv
    rb = rh * eb
    s0 = state[...]
    n_mat = jnp.where(strict, mm(al, be, BNT), 0.0)
    a_k = jnp.where(strict, mm(al, kb, BNT), 0.0)
    t_inv = _unit_lower_inverse(n_mat, chunk)
    u = _mm(t_inv, mm(al, s0, BNT) + mm(a_k, vh, BNN), BNN, pa=2, pb=2)
    y = (mm(rb, s0, BNT) + mm(jnp.where(lower, mm(rb, be, BNT), 0.0), u, BNN)
         + mm(jnp.where(lower, mm(rb, kb, BNT), 0.0), vh, BNN))
    state[...] = (s0 + mm(u, be, BTN) + mm(vh, kb, BTN)) * eb[:, chunk - 1:chunk, :]
    mean = jnp.mean(y, axis=-1, keepdims=True)
    var = jnp.mean(jnp.square(y - mean), axis=-1, keepdims=True)
    yn = (y - mean) * lax.rsqrt(var + RW_GN_EPS) * head_param(lnw_ref) + head_param(lnb_ref)
    bonus = jnp.sum(rh * kh * head_param(rk_ref), axis=-1, keepdims=True) * vh
    out = ((yn + bonus) * heads(g)).reshape(bb, n_heads, chunk, HEAD_DIM)
    for h in range(n_heads):
        o_ref[:, :, h * HEAD_DIM:(h + 1) * HEAD_DIM] = out[:, h]

    s_ref[...] = state[...].reshape(s_ref.shape)


def _rwkv(p_rw, shift, s0, prm, passes):
    b, t, cols = p_rw.shape
    n_heads = s0.shape[1]
    width = n_heads * HEAD_DIM
    chunk = _divisor(t, 64)
    bb = _divisor(b, RWKV_SEQS, 1)
    vec = lambda n: pl.BlockSpec((1, n), lambda bi, ci: (0, 0))
    mat = lambda m, n: pl.BlockSpec((m, n), lambda bi, ci: (0, 0))
    return pl.pallas_call(
        functools.partial(_rwkv_kernel, chunk=chunk, n_heads=n_heads, passes=passes, bb=bb),
        grid=(b // bb, t // chunk),
        in_specs=[
            pl.BlockSpec((bb, chunk, cols), lambda bi, ci: (bi, ci, 0)),
            pl.BlockSpec((bb, 1, cols), lambda bi, ci: (bi, 0, 0)),
            pl.BlockSpec((bb, n_heads, HEAD_DIM, HEAD_DIM), lambda bi, ci: (bi, 0, 0, 0)),
            vec(cols), vec(width), mat(LANE, width), vec(width), mat(LANE, width), mat(2 * LANE, width),
            vec(width), vec(width), vec(width), vec(width), vec(width),
        ],
        out_specs=[
            pl.BlockSpec((bb, chunk, width), lambda bi, ci: (bi, ci, 0)),
            pl.BlockSpec((bb, n_heads, HEAD_DIM, HEAD_DIM), lambda bi, ci: (bi, 0, 0, 0)),
        ],
        out_shape=[
            jax.ShapeDtypeStruct((b, t, width), F32),
            jax.ShapeDtypeStruct((b, n_heads, HEAD_DIM, HEAD_DIM), F32),
        ],
        scratch_shapes=[pltpu.VMEM((bb * n_heads, HEAD_DIM, HEAD_DIM), F32), pltpu.VMEM((bb, 1, cols), F32)],
        compiler_params=_cparams(("parallel", "arbitrary")),
        name="rwkv7",
    )(p_rw, shift, s0, prm["mu"], prm["w0"], prm["w_up"], prm["a0"], prm["a_up"], prm["g_up"],
      prm["k_k"], prm["k_a"], prm["r_k"], prm["ln_w"], prm["ln_b"])


HGRN_SEQS = 8


def _hgrn_kernel(p_ref, s0_ref, lbl_ref, ng_ref, o_ref, s_ref, st, *, chunk, n_heads, layer, passes, bb):
    c = pl.program_id(1)
    width = n_heads * HEAD_DIM
    mm = functools.partial(_mm, pa=passes, pb=passes)
    head_r = _iota((width, width), 0) // HEAD_DIM
    head_c = _iota((width, width), 1) // HEAD_DIM
    same_head = head_r == head_c
    ones_bd = jnp.where(same_head, 1.0, 0.0).astype(BF16)

    @pl.when(c == 0)
    def _():
        st[...] = jnp.zeros_like(st)
        for s in range(bb):
            for h in range(n_heads):
                lanes = slice(h * HEAD_DIM, (h + 1) * HEAD_DIM)
                st[s, lanes, lanes] = s0_ref[s, h].T

    logits = lbl_ref[...]
    e = jnp.exp(logits - jnp.max(logits, axis=0, keepdims=True))
    sm = e / jnp.sum(e, axis=0, keepdims=True)
    lb = jnp.zeros((1, width), F32)
    for j in range(1, layer + 1):
        lb = lb + sm[j:j + 1, :]

    p = p_ref[...]
    fp = p[:, :, 0:width]
    iv = p[:, :, width:2 * width]
    q = p[:, :, 2 * width:3 * width]
    go = p[:, :, 3 * width:4 * width]
    x1 = jnp.log(lb)
    x2 = jnp.log1p(-lb) + _log_sigmoid(fp)
    log_f = jnp.maximum(x1, x2) + jnp.log1p(jnp.exp(-jnp.abs(x1 - x2)))
    key = (1.0 - lb) * jax.nn.sigmoid(-fp)

    row = _iota((chunk, chunk), 0)
    col = _iota((chunk, chunk), 1)
    incl = jnp.broadcast_to(jnp.where(col <= row, 1.0, 0.0).astype(BF16), (bb, chunk, chunk))
    b = _mm(incl, log_f, BNN, pb=3)
    b_end = b[:, chunk - 1:chunk, :]

    st0 = st[...]
    inter = mm(q * jnp.exp(b), st0, BNT)
    diff = b[:, None, :, :] - b[:, :, None, :]
    shape4 = (bb, chunk, chunk, width)
    pair = jnp.where(_iota(shape4, 1) <= _iota(shape4, 2),
                     jnp.exp(jnp.minimum(diff, 0.0)) * q[:, None, :, :] * key[:, :, None, :], 0.0)
    att = _mm(pair.reshape(bb * chunk * chunk, width), ones_bd, pa=2).reshape(shape4)
    intra = jnp.sum(att * iv[:, :, None, :], axis=1)
    st[...] = st0 * jnp.exp(b_end) + jnp.where(same_head, mm(iv, key * jnp.exp(b_end - b), BTN), 0.0)

    o = inter + intra
    ms = _mm((o * o).reshape(bb * chunk, width), ones_bd, pa=2).reshape(o.shape) * (1.0 / HEAD_DIM)
    o_ref[...] = o * lax.rsqrt(ms + NORM_EPS) * ng_ref[...] * jax.nn.sigmoid(go)

    @pl.when(c == pl.num_programs(1) - 1)
    def _():
        for s in range(bb):
            for h in range(n_heads):
                lanes = slice(h * HEAD_DIM, (h + 1) * HEAD_DIM)
                s_ref[s, h] = st[s, lanes, lanes].T


def _hgrn(p_hg, s0, lb_logits, norm_g_tiled, layer, passes):
    b, t, cols = p_hg.shape
    n_heads = s0.shape[1]
    width = n_heads * HEAD_DIM
    chunk = _divisor(math.gcd(t, 64), 16)
    bb = _divisor(b, HGRN_SEQS, 1)
    depth = lb_logits.shape[0]
    return pl.pallas_call(
        functools.partial(_hgrn_kernel, chunk=chunk, n_heads=n_heads, layer=layer, passes=passes, bb=bb),
        grid=(b // bb, t // chunk),
        in_specs=[
            pl.BlockSpec((bb, chunk, cols), lambda bi, ci: (bi, ci, 0)),
            pl.BlockSpec((bb, n_heads, HEAD_DIM, HEAD_DIM), lambda bi, ci: (bi, 0, 0, 0)),
            pl.BlockSpec((depth, width), lambda bi, ci: (0, 0)),
            pl.BlockSpec((1, width), lambda bi, ci: (0, 0)),
        ],
        out_specs=[
            pl.BlockSpec((bb, chunk, width), lambda bi, ci: (bi, ci, 0)),
            pl.BlockSpec((bb, n_heads, HEAD_DIM, HEAD_DIM), lambda bi, ci: (bi, 0, 0, 0)),
        ],
        out_shape=[
            jax.ShapeDtypeStruct((b, t, width), F32),
            jax.ShapeDtypeStruct((b, n_heads, HEAD_DIM, HEAD_DIM), F32),
        ],
        scratch_shapes=[pltpu.VMEM((bb, width, width), F32)],
        compiler_params=_cparams(("parallel", "arbitrary")),
        name="hgrn2",
    )(p_hg, s0, lb_logits, norm_g_tiled)


def _merge_kernel(x_ref, gt_ref, sh_ref, sc_ref, osb_ref, orw_ref, ohg_ref, pg_ref, wsb_ref, wrw_ref, whg_ref,
                  wout_ref, ng_ref, rw_ref, rb_ref, xo_ref, h_ref, idx_ref, prob_ref, *, rows, d, passes):
    mm = functools.partial(_mm, pa=passes, pb=passes)

    def flat(ref):
        val = ref[...]
        return val.reshape(rows, val.shape[-1])

    gates = jax.nn.sigmoid(flat(pg_ref))
    merged = (gates[:, 0:d] * mm(flat(osb_ref), wsb_ref[...])
              + gates[:, d:2 * d] * mm(flat(orw_ref), wrw_ref[...])
              + gates[:, 2 * d:3 * d] * mm(flat(ohg_ref), whg_ref[...]))
    upd = mm(merged, wout_ref[...]).reshape(x_ref.shape)
    x = x_ref[...] + gt_ref[...] * upd
    xo_ref[...] = x
    ms = jnp.mean(x * x, axis=-1, keepdims=True)
    y = x * lax.rsqrt(ms + NORM_EPS) * ng_ref[...]
    h3 = y * (1.0 + sc_ref[...]) + sh_ref[...]
    h_ref[...] = h3.astype(h_ref.dtype)
    h = h3.reshape(rows, d)
    logits = _mm(h, rw_ref[...], pa=2, pb=2) + rb_ref[...]
    n_exp = logits.shape[-1]
    lane = _iota(logits.shape, 1)
    work = logits
    picks, vals = [], []
    for _ in range(TOP_K):
        m = jnp.max(work, axis=-1, keepdims=True)
        idx = jnp.min(jnp.where(work == m, lane, n_exp), axis=-1, keepdims=True)
        picks.append(idx)
        vals.append(m)
        work = jnp.where(lane == idx, -jnp.inf, work)
    exps = [jnp.exp(vv - vals[0]) for vv in vals]
    denom = exps[0]
    for ee in exps[1:]:
        denom = denom + ee
    out_lane = _iota((rows, LANE), 1)
    idx_out = jnp.zeros((rows, LANE), jnp.int32)
    prob_out = jnp.zeros((rows, LANE), F32)
    for kk, (idx, ee) in enumerate(zip(picks, exps)):
        idx_out = jnp.where(out_lane == kk, idx, idx_out)
        prob_out = jnp.where(out_lane == kk, ee / denom, prob_out)
    idx_ref[...] = idx_out.reshape(idx_ref.shape)
    prob_ref[...] = prob_out.reshape(prob_ref.shape)


def _merge(x3, gt, sh, sc, o_sb, o_rw, o_hg, p_gate, w_sb, w_rw, w_hg, w_out, norm_g, router_w, router_b,
           passes):
    b, t, d = x3.shape
    bb, tt = _row_tiles(b, t, 256)
    n_exp = router_w.shape[1]
    tile = lambda w: pl.BlockSpec((bb, tt, w), lambda i, j: (i, j, 0))
    per_seq = pl.BlockSpec((bb, 1, d), lambda i, j: (i, 0, 0))
    whole = lambda a: pl.BlockSpec(a.shape, lambda i, j: (0,) * a.ndim)
    ng = norm_g.reshape(1, 1, d)
    rb = router_b.reshape(1, n_exp)
    return pl.pallas_call(
        functools.partial(_merge_kernel, rows=bb * tt, d=d, passes=passes),
        grid=(b // bb, t // tt),
        in_specs=[tile(d), per_seq, per_seq, per_seq, tile(o_sb.shape[-1]), tile(o_rw.shape[-1]),
                  tile(o_hg.shape[-1]), tile(p_gate.shape[-1]), whole(w_sb), whole(w_rw), whole(w_hg),
                  whole(w_out), whole(ng), whole(router_w), whole(rb)],
        out_specs=[tile(d), tile(d), tile(LANE), tile(LANE)],
        out_shape=[jax.ShapeDtypeStruct((b, t, d), F32), jax.ShapeDtypeStruct((b, t, d), F32),
                   jax.ShapeDtypeStruct((b, t, LANE), jnp.int32), jax.ShapeDtypeStruct((b, t, LANE), F32)],
        compiler_params=_cparams(("parallel", "parallel")),
        name="merge",
    )(x3, gt, sh, sc, o_sb, o_rw, o_hg, p_gate, w_sb, w_rw, w_hg, w_out, ng, router_w, rb)


MOE_TILE = 256
RANK_TILE = 512
PAIR_TILE = 256


def _rank_kernel(e_ref, rank_ref, count_ref, carry, *, n_exp):
    i = pl.program_id(0)

    @pl.when(i == 0)
    def _():
        carry[...] = jnp.zeros_like(carry)

    e = e_ref[0]
    n = e.shape[1]
    hit = e == _iota((n_exp, n), 0)
    onehot = jnp.where(hit, 1.0, 0.0)
    earlier = jnp.where(_iota((n, n), 0) < _iota((n, n), 1), 1.0, 0.0).astype(BF16)
    before = lax.dot_general(onehot.astype(BF16), earlier, NN, preferred_element_type=F32) + carry[...]
    rank_ref[0] = jnp.sum(jnp.where(hit, before, 0.0), axis=0, keepdims=True).astype(jnp.int32)
    carry[...] = carry[...] + jnp.sum(onehot, axis=1, keepdims=True)
    count_ref[...] = carry[...].astype(jnp.int32)


def _rank(e_flat, n_exp):
    p = e_flat.shape[0]
    tile = _divisor(p, RANK_TILE, LANE)
    rank, count = pl.pallas_call(
        functools.partial(_rank_kernel, n_exp=n_exp),
        grid=(p // tile,),
        in_specs=[pl.BlockSpec((1, 1, tile), lambda i: (i, 0, 0))],
        out_specs=[pl.BlockSpec((1, 1, tile), lambda i: (i, 0, 0)), pl.BlockSpec((n_exp, 1), lambda i: (0, 0))],
        out_shape=[jax.ShapeDtypeStruct((p // tile, 1, tile), jnp.int32),
                   jax.ShapeDtypeStruct((n_exp, 1), jnp.int32)],
        scratch_shapes=[pltpu.VMEM((n_exp, 1), F32)],
        compiler_params=_cparams(("arbitrary",)),
        name="moe_rank",
    )(e_flat.reshape(p // tile, 1, tile))
    return rank.reshape(p), count.reshape(n_exp)


def _row_copies_wait(rows_ref, sem):
    pltpu.make_async_copy(rows_ref, rows_ref, sem).wait()


def _dispatch_kernel(slot_hbm, h_ref, init_hbm, out_hbm, slot_smem, sem_idx, sem_rows, *, tokens):
    del init_hbm
    i = pl.program_id(0)
    pairs = tokens * TOP_K
    idx_copy = pltpu.make_async_copy(slot_hbm.at[pl.ds(i * pairs, pairs)], slot_smem, sem_idx)
    idx_copy.start()
    idx_copy.wait()

    def issue(r, carry):
        for kk in range(TOP_K):
            slot = slot_smem[r * TOP_K + kk]
            pltpu.make_async_copy(h_ref.at[pl.ds(r, 1), :], out_hbm.at[pl.ds(slot, 1), :],
                                  sem_rows).start(priority=kk % 2)
        return carry

    lax.fori_loop(0, tokens, issue, 0)
    for _ in range(TOP_K):
        _row_copies_wait(h_ref, sem_rows)


def _dispatch(h, slots, n_slots):
    n, d = h.shape
    tokens = _divisor(n, PAIR_TILE)
    return pl.pallas_call(
        functools.partial(_dispatch_kernel, tokens=tokens),
        grid=(n // tokens,),
        in_specs=[
            pl.BlockSpec(memory_space=pl.ANY),
            pl.BlockSpec((tokens, d), lambda i: (i, 0)),
            pl.BlockSpec(memory_space=pl.ANY),
        ],
        out_specs=pl.BlockSpec(memory_space=pl.ANY),
        out_shape=jax.ShapeDtypeStruct((n_slots, d), F32),
        scratch_shapes=[pltpu.SMEM((tokens * TOP_K,), jnp.int32), pltpu.SemaphoreType.DMA(()),
                        pltpu.SemaphoreType.DMA(())],
        input_output_aliases={2: 0},
        compiler_params=_cparams(("arbitrary",)),
        name="moe_dispatch",
    )(slots, h, jnp.zeros((n_slots, d), F32))


def _experts_kernel(te_ref, x_ref, w1_ref, b1_ref, w2_ref, b2_ref, o_ref, w1b, w2b, *, d_exp, n_used_idx):
    i = pl.program_id(0)
    expert = te_ref[i]
    fresh = jnp.logical_or(i == 0, expert != te_ref[jnp.maximum(i - 1, 0)])

    @pl.when(fresh)
    def _():
        rows = w1b.shape[0] // 8
        for c in range(8):
            w1b[c * rows:(c + 1) * rows, :] = w1_ref[0, c * rows:(c + 1) * rows, :].astype(BF16)
        rows = w2b.shape[0] // 8
        for c in range(8):
            w2b[c * rows:(c + 1) * rows, :] = w2_ref[0, c * rows:(c + 1) * rows, :].astype(BF16)

    used = i < te_ref[n_used_idx]

    @pl.when(used)
    def _():
        gu = lax.dot_general(x_ref[...].astype(BF16), w1b[...], NN, preferred_element_type=F32) + b1_ref[0]
        glu = jnp.minimum(gu[:, :d_exp], SWIGLU_LIMIT)
        lin = jnp.clip(gu[:, d_exp:], -SWIGLU_LIMIT, SWIGLU_LIMIT)
        act = glu * jax.nn.sigmoid(SWIGLU_ALPHA * glu) * (lin + 1.0)
        o_ref[...] = lax.dot_general(act.astype(BF16), w2b[...], NN, preferred_element_type=F32) + b2_ref[0]

    @pl.when(jnp.logical_not(used))
    def _():
        o_ref[...] = jnp.zeros_like(o_ref)


def _experts(x_slots, tile_expert, w1, b1, w2, b2, first_expert):
    s, d = x_slots.shape
    n_exp, _, two_de = w1.shape
    d_exp = two_de // 2
    n_tiles = s // MOE_TILE
    grid_spec = pltpu.PrefetchScalarGridSpec(
        num_scalar_prefetch=1,
        grid=(n_tiles,),
        in_specs=[
            pl.BlockSpec((MOE_TILE, d), lambda i, te: (i, 0)),
            pl.BlockSpec((1, d, two_de), lambda i, te: (first_expert + te[i], 0, 0)),
            pl.BlockSpec((1, 1, two_de), lambda i, te: (first_expert + te[i], 0, 0)),
            pl.BlockSpec((1, d_exp, d), lambda i, te: (first_expert + te[i], 0, 0)),
            pl.BlockSpec((1, 1, d), lambda i, te: (first_expert + te[i], 0, 0)),
        ],
        out_specs=pl.BlockSpec((MOE_TILE, d), lambda i, te: (i, 0)),
        scratch_shapes=[pltpu.VMEM((d, two_de), BF16), pltpu.VMEM((d_exp, d), BF16)],
    )
    return pl.pallas_call(
        functools.partial(_experts_kernel, d_exp=d_exp, n_used_idx=n_tiles),
        grid_spec=grid_spec,
        out_shape=jax.ShapeDtypeStruct((s, d), F32),
        compiler_params=_cparams(("arbitrary",)),
        name="moe_experts",
    )(tile_expert, x_slots, w1, b1.reshape(n_exp, 1, two_de), w2, b2.reshape(n_exp, 1, d))


def _combine_kernel(slot_hbm, y_hbm, prob_ref, o_ref, slot_smem, sem_idx, sem_rows, rows_buf, *, tokens, base):
    i = pl.program_id(0)
    pairs = tokens * TOP_K
    idx_copy = pltpu.make_async_copy(slot_hbm.at[pl.ds(base + i * pairs, pairs)], slot_smem, sem_idx)
    idx_copy.start()
    idx_copy.wait()

    def issue(r, carry):
        for kk in range(TOP_K):
            slot = slot_smem[r * TOP_K + kk]
            pltpu.make_async_copy(y_hbm.at[pl.ds(slot, 1), :], rows_buf.at[kk, pl.ds(r, 1), :],
                                  sem_rows).start(priority=kk % 2)
        return carry

    lax.fori_loop(0, tokens, issue, 0)
    _row_copies_wait(rows_buf, sem_rows)
    prob = prob_ref[...]
    acc = prob[:, 0:1] * rows_buf[0]
    for kk in range(1, TOP_K):
        acc = acc + prob[:, kk:kk + 1] * rows_buf[kk]
    o_ref[...] = acc


def _combine(y_slots, slots, prob, first_token):
    n = prob.shape[0]
    d = y_slots.shape[1]
    tokens = _divisor(n, PAIR_TILE)
    return pl.pallas_call(
        functools.partial(_combine_kernel, tokens=tokens, base=first_token * TOP_K),
        grid=(n // tokens,),
        in_specs=[
            pl.BlockSpec(memory_space=pl.ANY),
            pl.BlockSpec(memory_space=pl.ANY),
            pl.BlockSpec((tokens, LANE), lambda i: (i, 0)),
        ],
        out_specs=pl.BlockSpec((tokens, d), lambda i: (i, 0)),
        out_shape=jax.ShapeDtypeStruct((n, d), F32),
        scratch_shapes=[pltpu.SMEM((tokens * TOP_K,), jnp.int32), pltpu.SemaphoreType.DMA(()),
                        pltpu.SemaphoreType.DMA(()), pltpu.VMEM((TOP_K, tokens, d), F32)],
        compiler_params=_cparams(("arbitrary",)),
        name="moe_combine",
    )(slots, y_slots, prob)


def _moe_plan(idx, n_exp):
    n = idx.shape[0]
    pairs = n * TOP_K
    e_flat = idx.reshape(pairs)
    rank, count = _rank(e_flat, n_exp)
    padded = ((count + MOE_TILE - 1) // MOE_TILE) * MOE_TILE
    group_end = jnp.cumsum(padded)
    group_start = group_end - padded
    slots = (jnp.take(group_start, e_flat) + rank).astype(jnp.int32)
    n_slots = ((pairs + MOE_TILE - 1) // MOE_TILE) * MOE_TILE + n_exp * MOE_TILE
    tile_start = jnp.arange(n_slots // MOE_TILE, dtype=jnp.int32) * MOE_TILE
    owner = jnp.sum((tile_start[:, None] >= group_end[None, :]).astype(jnp.int32), axis=1)
    tile_expert = jnp.minimum(owner, n_exp - 1).astype(jnp.int32)
    tiles_used = (group_end[-1] // MOE_TILE).astype(jnp.int32)
    return slots, jnp.concatenate([tile_expert, tiles_used[None]]), n_slots


def _pad_cols(a, n):
    return jnp.pad(a, [(0, 0)] * (a.ndim - 1) + [(0, n - a.shape[-1])])


def _pad_rows(a, n):
    return jnp.pad(a, [(0, n - a.shape[0]), (0, 0)])


def _rw_layout(a, rw_width):
    o1 = 3 * rw_width
    o2 = o1 + RW_DECAY_LORA
    o3 = o2 + RW_AAA_LORA
    return jnp.concatenate([
        a[..., :o1], _pad_cols(a[..., o1:o2], LANE), _pad_cols(a[..., o2:o3], LANE),
        _pad_cols(a[..., o3:], 2 * LANE)], axis=-1)


def _rw_unlayout(a, rw_width):
    o1 = 3 * rw_width
    return jnp.concatenate([
        a[..., :o1], a[..., o1:o1 + RW_DECAY_LORA], a[..., o1 + LANE:o1 + LANE + RW_AAA_LORA],
        a[..., o1 + 2 * LANE:o1 + 2 * LANE + RW_GATE_LORA]], axis=-1)


def _mixer_half(grp, l, w, mod_l):
    x = grp["x"]
    b, t, d = x.shape
    sb_w, rw_w = w["sb_w"], w["rw_w"]
    m = mod_l.reshape(b, 1, 6 * d)
    sh_a, sc_a, gt_a, sh_f, sc_f, gt_f = [m[:, :, i * d:(i + 1) * d] for i in range(6)]
    passes = w["passes"]
    delta = grp["delta"]
    x_new, h = _norm(x, delta, grp["gt_prev"], sh_a, sc_a, w["norm_attn_g"], emit_x=delta is not None,
                     h_dtype=BF16 if passes == 1 else F32)
    if x_new is not None:
        x = x_new
    hf = h.reshape(b * t, d)
    q, k, v = [a.reshape(b, t, sb_w) for a in _proj(hf, w["w_sb"], (sb_w, sb_w, sb_w), passes)]
    (p_rw,) = _proj(hf, w["w_rw"], (w["w_rw"].shape[1],), passes)
    (p_hg,) = _proj(hf, w["w_hg"], (w["w_hg"].shape[1],), passes)
    (p_gate,) = _proj(hf, w["w_gate"], (w["w_gate"].shape[1],), passes)
    p_rw = p_rw.reshape(b, t, -1)
    p_hg = p_hg.reshape(b, t, -1)
    p_gate = p_gate.reshape(b, t, -1)
    o_sb = grp["attn"](l, q, k, v, w["sb_bias"])
    o_rw, s_rw = _rwkv(p_rw, _rw_layout(grp["rw_shift"][l], rw_w).reshape(b, 1, -1), grp["rw_state"][l], w["rw"],
                       passes)
    o_hg, s_hg = _hgrn(p_hg, grp["hg_state"][l], w["hg_lb_logits"], w["hg_norm_g"], l, passes)
    x, h2, idx, prob = _merge(x, gt_a, sh_f, sc_f, o_sb, o_rw, o_hg, p_gate, w["w_branch_sb"], w["w_branch_rw"],
                              w["w_branch_hg"], w["w_out"], w["norm_ffn_g"], w["router_w"], w["router_b"], passes)
    grp["x"] = x
    grp["gt_prev"] = gt_f
    grp["ks"].append(k.reshape(b, t, sb_w // HEAD_DIM, HEAD_DIM))
    grp["vs"].append(v.reshape(b, t, sb_w // HEAD_DIM, HEAD_DIM))
    grp["rws"].append(s_rw)
    grp["shs"].append(_rw_unlayout(p_rw[:, -1], rw_w))
    grp["hgs"].append(s_hg)
    return h2.reshape(b * t, d), idx.reshape(b * t, LANE)[:, :TOP_K], prob.reshape(b * t, LANE)


def _run_layers(groups, mods, lw, moe, depth):
    n_exp = moe["w1"].shape[0] // depth
    for l in range(depth):
        halves = [_mixer_half(grp, l, lw[l], mod[l]) for grp, mod in zip(groups, mods)]
        h_all = jnp.concatenate([hh[0] for hh in halves], axis=0)
        idx_all = jnp.concatenate([hh[1] for hh in halves], axis=0)
        slots, tile_expert, n_slots = _moe_plan(idx_all, n_exp)
        x_slots = _dispatch(h_all, slots, n_slots)
        y_slots = _experts(x_slots, tile_expert, moe["w1"], moe["b1"], moe["w2"], moe["b2"], l * n_exp)
        first = 0
        for grp, hh in zip(groups, halves):
            b, t, d = grp["x"].shape
            grp["delta"] = _combine(y_slots, slots, hh[2], first).reshape(b, t, d)
            first += b * t
    outs = []
    for grp in groups:
        b, t, d = grp["x"].shape
        zeros = jnp.zeros((b, 1, d), F32)
        _, y = _norm(grp["x"], grp["delta"], grp["gt_prev"], zeros, zeros, lw[0]["final_norm_g"], emit_x=False,
                     h_dtype=F32)
        outs.append((y, jnp.stack(grp["ks"]), jnp.stack(grp["vs"]), jnp.stack(grp["rws"]), jnp.stack(grp["shs"]),
                     jnp.stack(grp["hgs"])))
    return outs


def kernel(x_prompt, x_sample, cache_sb_k, cache_sb_v, state_rwkv, state_rwkv_shift, state_hgrn, page_table,
           c_prompt, c_sample, w_ada, b_ada, norm_attn_g, norm_ffn_g, w_in, sb_bias, rw_mu, rw_w0, rw_w_up, rw_a0,
           rw_a_up, rw_g_up, rw_k_k, rw_k_a, rw_r_k, rw_ln_w, rw_ln_b, hg_lb_logits, hg_norm_g, w_branch_sb,
           w_branch_rw, w_branch_hg, w_out, router_w, router_b, moe_w1, moe_b1, moe_w2, moe_b2, final_norm_g):
    depth, d, _ = w_in.shape
    bp = x_prompt.shape[0]
    db = x_sample.shape[0]
    sb_w = w_branch_sb.shape[1]
    rw_w = w_branch_rw.shape[1]
    hg_w = w_branch_hg.shape[1]
    rw_heads = rw_w // HEAD_DIM
    hg_heads = hg_w // HEAD_DIM
    rw_cols = 3 * rw_w + RW_DECAY_LORA + RW_AAA_LORA + RW_GATE_LORA
    o_rw = 3 * sb_w
    o_hg = o_rw + rw_cols
    o_gate = o_hg + 4 * hg_w

    mod = _ada(jnp.concatenate([c_prompt, c_sample], axis=0), w_ada, b_ada)

    lw = []
    for l in range(depth):
        wl = w_in[l]
        passes = 2 if l + 1 < depth else 1
        wdt = BF16 if passes == 1 else F32
        lw.append(dict(
            sb_w=sb_w, rw_w=rw_w, passes=passes,
            w_sb=wl[:, :o_rw].astype(wdt),
            w_rw=_rw_layout(wl[:, o_rw:o_hg], rw_w).astype(wdt),
            w_hg=wl[:, o_hg:o_gate].astype(wdt),
            w_gate=wl[:, o_gate:].astype(wdt),
            norm_attn_g=norm_attn_g[l], norm_ffn_g=norm_ffn_g[l], final_norm_g=final_norm_g,
            sb_bias=sb_bias[l],
            rw=dict(
                mu=_rw_layout(rw_mu[l][None, :], rw_w), w0=rw_w0[l][None, :],
                w_up=_pad_rows(rw_w_up[l], LANE), a0=rw_a0[l][None, :], a_up=_pad_rows(rw_a_up[l], LANE),
                g_up=_pad_rows(rw_g_up[l], 2 * LANE), k_k=rw_k_k[l][None, :], k_a=rw_k_a[l][None, :],
                r_k=rw_r_k[l].reshape(1, rw_w), ln_w=rw_ln_w[l][None, :], ln_b=rw_ln_b[l][None, :]),
            hg_lb_logits=hg_lb_logits,
            hg_norm_g=jnp.tile(hg_norm_g[l], hg_heads)[None, :],
            w_branch_sb=w_branch_sb[l].astype(wdt), w_branch_rw=w_branch_rw[l].astype(wdt),
            w_branch_hg=w_branch_hg[l].astype(wdt), w_out=w_out[l].astype(wdt),
            router_w=router_w[l], router_b=router_b[l],
        ))
    n_exp = moe_w1.shape[1]
    moe = dict(w1=moe_w1.reshape((depth * n_exp,) + moe_w1.shape[2:]), b1=moe_b1.reshape(depth * n_exp, -1),
               w2=moe_w2.reshape((depth * n_exp,) + moe_w2.shape[2:]), b2=moe_b2.reshape(depth * n_exp, -1))

    def prompt_attn(l, q, k, v, bias):
        del l
        return _sb_prompt(q, k, v, bias)

    cache_k_t = jnp.transpose(cache_sb_k, (0, 1, 3, 4, 2))
    cache_v_t = jnp.transpose(cache_sb_v, (0, 1, 3, 4, 2))

    def sample_attn(l, q, k, v, bias):
        return _sb_sample(q, k, v, cache_k_t, cache_v_t, page_table, bias, l)

    zeros_rw = jnp.zeros((depth, bp, rw_heads, HEAD_DIM, HEAD_DIM), F32)
    zeros_sh = jnp.zeros((depth, bp, rw_cols), F32)
    zeros_hg = jnp.zeros((depth, bp, hg_heads, HEAD_DIM, HEAD_DIM), F32)

    def group(x, attn, rw_state, rw_shift, hg_state):
        return dict(x=x, attn=attn, rw_state=rw_state, rw_shift=rw_shift, hg_state=hg_state, delta=None,
                    gt_prev=None, ks=[], vs=[], rws=[], shs=[], hgs=[])

    groups = [group(x_prompt, prompt_attn, zeros_rw, zeros_sh, zeros_hg),
              group(x_sample, sample_attn, state_rwkv, state_rwkv_shift, state_hgrn)]
    out_p, out_s = _run_layers(groups, [mod[:, :bp], mod[:, bp:]], lw, moe, depth)
    return (out_p[0], out_s[0]) + tuple(out_p[1:]) + tuple(out_s[1:])
```
